```python
import math
import jax, jax.numpy as jnp
from jax import lax
import numpy as np

D_MODEL = 1024
BATCH = 8
SEQ = 4096
DEPTH = 1

N_META = 16
DN_HEADS = 4
DN_HEAD_DIM = 128
DN_WIDTH = DN_HEADS * DN_HEAD_DIM
DN_CONV = 4
DN_CHUNK = 64
SB_HEADS = 8
SB_HEAD_DIM = 64
SB_WIDTH = SB_HEADS * SB_HEAD_DIM
SB_BLOCK = 128
N_BRANCH = 2
IN_COLS = 4 * DN_WIDTH + 2 * DN_HEADS + 3 * SB_WIDTH + N_BRANCH * D_MODEL
SPLIT_POINTS = [3 * DN_WIDTH, 4 * DN_WIDTH, 4 * DN_WIDTH + DN_HEADS, 4 * DN_WIDTH + 2 * DN_HEADS,
                4 * DN_WIDTH + 2 * DN_HEADS + 3 * SB_WIDTH]
COL_DN_V0 = 2 * DN_WIDTH
COL_SB_V0 = 4 * DN_WIDTH + 2 * DN_HEADS + 2 * SB_WIDTH
N_GROUPS = 4
EXPERTS_PER_GROUP = 8
TOP_K_IN_GROUP = 2
EXPERT_FF = 256
DEEPNORM_ALPHA = (2.0 * DEPTH) ** 0.25
DEEPNORM_BETA = (8.0 * DEPTH) ** -0.25
LN_EPS = 1e-5
RMS_EPS = 1e-6

kernel_name = "hybrid_gdn_stickbreak_hiermoe_block"


def layer_norm(x, g, b):
    xf = x.astype(jnp.float32)
    mu = jnp.mean(xf, -1, keepdims=True)
    var = jnp.mean(jnp.square(xf - mu), -1, keepdims=True)
    return ((xf - mu) * lax.rsqrt(var + LN_EPS) * g.astype(jnp.float32) + b.astype(jnp.float32)).astype(x.dtype)


def rms_norm(x, g):
    xf = x.astype(jnp.float32)
    return xf * lax.rsqrt(jnp.mean(xf * xf, -1, keepdims=True) + RMS_EPS) * g.astype(jnp.float32)


def l2_normalize(x):
    xf = x.astype(jnp.float32)
    return xf * lax.rsqrt(jnp.sum(xf * xf, -1, keepdims=True) + RMS_EPS)


def split_heads(t, n):
    b, l, _ = t.shape
    return t.reshape(b, l, n, -1).transpose(0, 2, 1, 3)


def merge_heads(t):
    b, h, l, d = t.shape
    return t.transpose(0, 2, 1, 3).reshape(b, l, h * d)


def causal_depthwise_conv(x, w):
    c = x.shape[-1]
    return lax.conv_general_dilated(x, w[:, None, :], window_strides=(1,), padding=[(w.shape[0] - 1, 0)],
                                    dimension_numbers=("NWC", "WIO", "NWC"), feature_group_count=c)


def gated_delta_rule_chunked(q, k, v, beta, g):
    b, h, l, dk = q.shape
    dv = v.shape[-1]
    c = DN_CHUNK
    n = l // c
    q = q.reshape(b, h, n, c, dk)
    k = k.reshape(b, h, n, c, dk)
    v = v.reshape(b, h, n, c, dv)
    beta = beta.reshape(b, h, n, c)
    decay = jnp.cumsum(g.reshape(b, h, n, c), axis=-1)
    idx = jnp.arange(c)
    causal = idx[:, None] >= idx[None, :]
    strict = idx[:, None] > idx[None, :]
    diff = decay[..., :, None] - decay[..., None, :]
    lmask = jnp.where(causal, jnp.exp(jnp.where(causal, diff, 0.0)), 0.0)
    k_beta = k * beta[..., None]
    v_beta = v * beta[..., None]
    a = jnp.where(strict, jnp.einsum("bhnid,bhnjd->bhnij", k_beta, k) * lmask, 0.0)
    m = a + jnp.eye(c, dtype=a.dtype)
    u = lax.linalg.triangular_solve(m, v_beta, left_side=True, lower=True, unit_diagonal=True)
    w = lax.linalg.triangular_solve(m, k_beta * jnp.exp(decay)[..., None], left_side=True, lower=True,
                                    unit_diagonal=True)
    qk = jnp.where(causal, jnp.einsum("bhnid,bhnjd->bhnij", q, k) * lmask, 0.0)
    q_dec = q * jnp.exp(decay)[..., None]
    k_dec = k * jnp.exp(decay[..., -1:] - decay)[..., None]
    chunk_decay = jnp.exp(decay[..., -1])
    xs = tuple(jnp.moveaxis(t, 2, 0) for t in (q_dec, k_dec, u, w, qk, chunk_decay))

    def step(state, inp):
        qd, kd, uc, wc, qkc, cd = inp
        v_new = uc - jnp.einsum("bhcd,bhde->bhce", wc, state)
        o = jnp.einsum("bhcd,bhde->bhce", qd, state) + jnp.einsum("bhij,bhje->bhie", qkc, v_new)
        state = state * cd[..., None, None] + jnp.einsum("bhcd,bhce->bhde", kd, v_new)
        return state, o

    s0 = jnp.zeros((b, h, dk, dv), jnp.float32)
    _, o = lax.scan(step, s0, xs)
    return jnp.moveaxis(o, 0, 2).reshape(b, h, l, dv)


def stick_breaking_attention(q, k, v):
    l = q.shape[2]
    scale = q.shape[-1] ** -0.5
    bounds = [0] + list(range(N_META, l, SB_BLOCK)) + [l]
    outs = []
    for s0, s1 in zip(bounds[:-1], bounds[1:]):
        z = jnp.einsum("bhqd,bhkd->bhqk", q[:, :, s0:s1], k[:, :, :s1]).astype(jnp.float32) * scale
        mask = jnp.arange(s1)[None, :] < jnp.arange(s0, s1)[:, None]
        log_keep = jnp.where(mask, jax.nn.log_sigmoid(-z), 0.0)
        later = lax.cumsum(log_keep, axis=3, reverse=True) - log_keep
        weights = jnp.where(mask, jnp.exp(jax.nn.log_sigmoid(z) + later), 0.0)
        outs.append(jnp.einsum("bhqk,bhkd->bhqd", weights.astype(v.dtype), v[:, :, :s1]))
    return jnp.concatenate(outs, axis=2)


def hybrid_mixer(x, w_in, b_gate, dn_conv_w, dn_a_log, dn_dt_bias, dn_norm_g, w_branch_dn, w_branch_sb, w_out):
    b, l, d = x.shape
    proj = x @ w_in
    dn_qkv, dn_z, dn_b, dn_a, sb_qkv, gates = jnp.split(proj, SPLIT_POINTS, axis=-1)
    dn_qkv = jax.nn.silu(causal_depthwise_conv(dn_qkv, dn_conv_w))
    q, k, v = jnp.split(dn_qkv, 3, axis=-1)
    q = l2_normalize(split_heads(q, DN_HEADS)) * (DN_HEAD_DIM ** -0.5)
    k = l2_normalize(split_heads(k, DN_HEADS))
    v = split_heads(v, DN_HEADS).astype(jnp.float32)
    beta = jax.nn.sigmoid(dn_b.astype(jnp.float32)).transpose(0, 2, 1)
    g = (-jnp.exp(dn_a_log.astype(jnp.float32))
         * jax.nn.softplus(dn_a.astype(jnp.float32) + dn_dt_bias.astype(jnp.float32))).transpose(0, 2, 1)
    pad = (DN_CHUNK - N_META % DN_CHUNK) % DN_CHUNK
    front = lambda t: jnp.pad(t, [(0, 0), (0, 0), (pad, 0)] + [(0, 0)] * (t.ndim - 3))
    o_dn = gated_delta_rule_chunked(front(q), front(k), front(v), front(beta), front(g))[:, :, pad:]
    o_dn = merge_heads(rms_norm(o_dn, dn_norm_g)) * jax.nn.silu(dn_z.astype(jnp.float32))
    sq, sk, sv = (split_heads(t, SB_HEADS) for t in jnp.split(sb_qkv, 3, axis=-1))
    o_sb = merge_heads(stick_breaking_attention(sq, sk, sv))
    gate = jax.nn.sigmoid(gates.reshape(b, l, N_BRANCH, d) + b_gate)
    merged = (gate[:, :, 0] * (o_dn.astype(x.dtype) @ w_branch_dn)
              + gate[:, :, 1] * (o_sb.astype(x.dtype) @ w_branch_sb))
    return merged @ w_out


def hierarchical_moe(h, router_group_w, router_group_b, router_expert_w, router_expert_b,
                     expert_w_gate, expert_w_up, expert_w_down):
    b, l, d = h.shape
    t = h.reshape(b * l, d)
    group_probs = jax.nn.softmax((t @ router_group_w).astype(jnp.float32) + router_group_b, axis=-1)
    g_idx = jnp.argmax(group_probs, axis=-1)
    g_prob = jnp.max(group_probs, axis=-1)
    exp_logits = jnp.einsum("td,gde->tge", t, router_expert_w).astype(jnp.float32) + router_expert_b
    sel_logits = jnp.take_along_axis(exp_logits, g_idx[:, None, None], axis=1)[:, 0]
    top_logit, top_idx = lax.top_k(sel_logits, TOP_K_IN_GROUP)
    top_w = jax.nn.softmax(top_logit, axis=-1) * g_prob[:, None]
    combine_e = jnp.sum(jax.nn.one_hot(top_idx, EXPERTS_PER_GROUP) * top_w[..., None], axis=1)
    combine = jax.nn.one_hot(g_idx, N_GROUPS)[:, :, None] * combine_e[:, None, :]
    y = jnp.zeros_like(t)
    for gi in range(N_GROUPS):
        hid = (jax.nn.silu(jnp.einsum("td,edf->tef", t, expert_w_gate[gi]))
               * jnp.einsum("td,edf->tef", t, expert_w_up[gi]))
        hid = hid * combine[:, gi, :, None].astype(hid.dtype)
        y = y + jnp.einsum("tef,efd->td", hid, expert_w_down[gi])
    return y.reshape(b, l, d)


def setup_inputs(seed: int = 0) -> dict:
    key = jax.random.key(seed)
    ks = jax.random.split(key, 24)
    f32 = jnp.float32
    nrm = lambda k, shape, scale: jax.random.normal(k, shape, f32) * scale
    col_scale = np.ones((IN_COLS,), np.float32)
    col_scale[COL_DN_V0:COL_DN_V0 + DN_WIDTH] = DEEPNORM_BETA
    col_scale[COL_SB_V0:COL_SB_V0 + SB_WIDTH] = DEEPNORM_BETA
    dt = jnp.exp(jax.random.uniform(ks[6], (DEPTH, DN_HEADS), f32, math.log(1e-3), math.log(1e-1)))
    return {
        "x": nrm(ks[0], (BATCH, SEQ, D_MODEL), 1.0),
        "meta_tokens": nrm(ks[1], (N_META, D_MODEL), 1.0),
        "ln_emb_g": 1.0 + nrm(ks[2], (D_MODEL,), 0.05),
        "ln_emb_b": nrm(ks[3], (D_MODEL,), 0.02),
        "w_in": nrm(ks[4], (DEPTH, D_MODEL, IN_COLS), D_MODEL ** -0.5) * jnp.asarray(col_scale),
        "b_gate": nrm(ks[5], (DEPTH, N_BRANCH, D_MODEL), 0.02),
        "dn_conv_w": nrm(ks[7], (DEPTH, DN_CONV, 3 * DN_WIDTH), DN_CONV ** -0.5),
        "dn_a_log": jnp.log(jax.random.uniform(ks[8], (DEPTH, DN_HEADS), f32, 1.0, 16.0)),
        "dn_dt_bias": jnp.log(jnp.expm1(dt)),
        "dn_norm_g": 1.0 + nrm(ks[9], (DEPTH, DN_HEAD_DIM), 0.05),
        "w_branch_dn": nrm(ks[10], (DEPTH, DN_WIDTH, D_MODEL), DEEPNORM_BETA * DN_WIDTH ** -0.5),
        "w_branch_sb": nrm(ks[11], (DEPTH, SB_WIDTH, D_MODEL), DEEPNORM_BETA * SB_WIDTH ** -0.5),
        "w_out": nrm(ks[12], (DEPTH, D_MODEL, D_MODEL), DEEPNORM_BETA * D_MODEL ** -0.5),
        "ln1_g": 1.0 + nrm(ks[13], (DEPTH, D_MODEL), 0.05),
        "ln1_b": nrm(ks[14], (DEPTH, D_MODEL), 0.02),
        "router_group_w": nrm(ks[15], (DEPTH, D_MODEL, N_GROUPS), D_MODEL ** -0.5),
        "router_group_b": nrm(ks[16], (DEPTH, N_GROUPS), 0.01),
        "router_expert_w": nrm(ks[17], (DEPTH, N_GROUPS, D_MODEL, EXPERTS_PER_GROUP), D_MODEL ** -0.5),
        "router_expert_b": nrm(ks[18], (DEPTH, N_GROUPS, EXPERTS_PER_GROUP), 0.01),
        "expert_w_gate": nrm(ks[19], (DEPTH, N_GROUPS, EXPERTS_PER_GROUP, D_MODEL, EXPERT_FF), D_MODEL ** -0.5),
        "expert_w_up": nrm(ks[20], (DEPTH, N_GROUPS, EXPERTS_PER_GROUP, D_MODEL, EXPERT_FF),
                           DEEPNORM_BETA * D_MODEL ** -0.5),
        "expert_w_down": nrm(ks[21], (DEPTH, N_GROUPS, EXPERTS_PER_GROUP, EXPERT_FF, D_MODEL),
                             DEEPNORM_BETA * EXPERT_FF ** -0.5),
        "ln2_g": 1.0 + nrm(ks[22], (DEPTH, D_MODEL), 0.05),
        "ln2_b": nrm(ks[23], (DEPTH, D_MODEL), 0.02),
    }


def reference(x, meta_tokens, ln_emb_g, ln_emb_b, w_in, b_gate, dn_conv_w, dn_a_log, dn_dt_bias, dn_norm_g,
              w_branch_dn, w_branch_sb, w_out, ln1_g, ln1_b, router_group_w, router_group_b, router_expert_w,
              router_expert_b, expert_w_gate, expert_w_up, expert_w_down, ln2_g, ln2_b):
    b = x.shape[0]
    meta = jnp.broadcast_to(meta_tokens[None].astype(x.dtype), (b, N_META, x.shape[-1]))
    h = layer_norm(jnp.concatenate([meta, x], axis=1), ln_emb_g, ln_emb_b)
    for i in range(DEPTH):
        mix = hybrid_mixer(h, w_in[i], b_gate[i], dn_conv_w[i], dn_a_log[i], dn_dt_bias[i], dn_norm_g[i],
                           w_branch_dn[i], w_branch_sb[i], w_out[i])
        h = layer_norm(DEEPNORM_ALPHA * h + mix, ln1_g[i], ln1_b[i])
        ffn = hierarchical_moe(h, router_group_w[i], router_group_b[i], router_expert_w[i], router_expert_b[i],
                               expert_w_gate[i], expert_w_up[i], expert_w_down[i])
        h = layer_norm(DEEPNORM_ALPHA * h + ffn, ln2_g[i], ln2_b[i])
    return h[:, N_META:]
```

```python
import functools

import jax
import jax.numpy as jnp
from jax import lax
from jax.experimental import pallas as pl
from jax.experimental.pallas import tpu as pltpu

F32 = jnp.float32
BF16 = jnp.bfloat16

D_MODEL = 1024
N_META = 16
DN_HEADS = 4
DN_HEAD_DIM = 128
DN_WIDTH = DN_HEADS * DN_HEAD_DIM
DN_CONV = 4
DN_CHUNK = 64
SB_HEADS = 8
SB_HEAD_DIM = 64
SB_WIDTH = SB_HEADS * SB_HEAD_DIM
N_GROUPS = 4
EXPERTS_PER_GROUP = 8
N_EXPERTS = N_GROUPS * EXPERTS_PER_GROUP
EXPERT_FF = 256
DEEPNORM_ALPHA = 2.0 ** 0.25
LN_EPS = 1e-5
RMS_EPS = 1e-6

LANES = 128
META_ROWS = DN_CHUNK
META_PAD = META_ROWS - N_META
ROUTER_COL0 = N_GROUPS
VMEM_LIMIT = 56 * 1024 * 1024

HIGHEST = lax.Precision.HIGHEST


def _layer_norm(x, g, b):
    mu = jnp.mean(x, -1, keepdims=True)
    xc = x - mu
    var = jnp.mean(xc * xc, -1, keepdims=True)
    return xc * lax.rsqrt(var + LN_EPS) * g + b


def _sigmoid(x):
    return 1.0 / (1.0 + jnp.exp(-x))


def _softplus(x):
    return jnp.maximum(x, 0.0) + jnp.log(1.0 + jnp.exp(-jnp.abs(x)))


def _silu(x):
    return x * _sigmoid(x)


def _dot(a, b):
    return jnp.dot(a, b, preferred_element_type=F32)


def _dot_nt(a, b):
    return lax.dot_general(a, b, (((1,), (1,)), ((), ())), preferred_element_type=F32)


def _dot_tn(a, b):
    return lax.dot_general(a, b, (((0,), (0,)), ((), ())), preferred_element_type=F32)


def _dot_f32(a, b):
    return jnp.dot(a, b, preferred_element_type=F32, precision=HIGHEST)


def _ln_proj_kernel(x_ref, g_ref, b_ref, wdn_ref, wsb_ref, wba_ref, dn_ref, sb_ref, ba_ref, *, n_zero):
    h = _layer_norm(x_ref[...], g_ref[...], b_ref[...])
    if n_zero:
        rows = lax.broadcasted_iota(jnp.int32, h.shape, 0)
        h = jnp.where(rows >= n_zero, h, 0.0)
    hb = h.astype(BF16)
    dn_ref[...] = _dot(hb, wdn_ref[...])
    sb_ref[...] = _dot(hb, wsb_ref[...]).astype(BF16)
    ba_ref[...] = _dot(hb, wba_ref[...])


def _ln_proj(x2, g, b, wdn, wsb, wba, *, tm, n_zero=0):
    rows = x2.shape[0]
    const = lambda i: (0, 0)
    row = lambda i: (i, 0)
    return pl.pallas_call(
        functools.partial(_ln_proj_kernel, n_zero=n_zero),
        grid=(rows // tm,),
        in_specs=[
            pl.BlockSpec((tm, D_MODEL), row),
            pl.BlockSpec((1, D_MODEL), const),
            pl.BlockSpec((1, D_MODEL), const),
            pl.BlockSpec(wdn.shape, const),
            pl.BlockSpec(wsb.shape, const),
            pl.BlockSpec(wba.shape, const),
        ],
        out_specs=[
            pl.BlockSpec((tm, 3 * DN_WIDTH), row),
            pl.BlockSpec((tm, 3 * SB_WIDTH), row),
            pl.BlockSpec((tm, LANES), row),
        ],
        out_shape=[
            jax.ShapeDtypeStruct((rows, 3 * DN_WIDTH), F32),
            jax.ShapeDtypeStruct((rows, 3 * SB_WIDTH), BF16),
            jax.ShapeDtypeStruct((rows, LANES), F32),
        ],
        compiler_params=pltpu.CompilerParams(dimension_semantics=("arbitrary",), vmem_limit_bytes=VMEM_LIMIT),
        name="ln_proj",
    )(x2, g, b, wdn, wsb, wba)


GDN_BLOCK = 256
CONV_HIST = 8


def _gdn_rows(src_ref, ba_ref, n, n_zero, o_ref, cw_ref, alog_ref, dtb_ref, ng_ref, xbuf, s_ref):
    c = DN_CHUNK
    xbuf[CONV_HIST:CONV_HIST + n, :] = src_ref[...]
    acc = xbuf[CONV_HIST:CONV_HIST + n, :] * cw_ref[DN_CONV - 1:DN_CONV, :]
    for i in range(DN_CONV - 1):
        s = DN_CONV - 1 - i
        acc = acc + xbuf[CONV_HIST - s:CONV_HIST - s + n, :] * cw_ref[i:i + 1, :]
    hist = xbuf[n:n + CONV_HIST, :]
    xbuf[0:CONV_HIST, :] = hist
    qkv = _silu(acc)

    ba = ba_ref[...]
    beta_all = _sigmoid(ba)
    g_all = -jnp.exp(alog_ref[...]) * _softplus(ba + dtb_ref[...])
    if n_zero:
        rows = lax.broadcasted_iota(jnp.int32, g_all.shape, 0)
        g_all = jnp.where(rows >= n_zero, g_all, 0.0)

    ri = lax.broadcasted_iota(jnp.int32, (c, c), 0)
    ci = lax.broadcasted_iota(jnp.int32, (c, c), 1)
    causal = ri >= ci
    strict = ri > ci
    tril = causal.astype(F32)
    eye = (ri == ci).astype(F32)

    for ch in range(n // c):
        r0 = ch * c
        dec = _dot_f32(tril, g_all[r0:r0 + c, :])
        dec_t = dec.T
        for h in range(DN_HEADS):
            q = qkv[r0:r0 + c, h * DN_HEAD_DIM:(h + 1) * DN_HEAD_DIM]
            k = qkv[r0:r0 + c, DN_WIDTH + h * DN_HEAD_DIM:DN_WIDTH + (h + 1) * DN_HEAD_DIM]
            v = qkv[r0:r0 + c, 2 * DN_WIDTH + h * DN_HEAD_DIM:2 * DN_WIDTH + (h + 1) * DN_HEAD_DIM]
            q = q * lax.rsqrt(jnp.sum(q * q, -1, keepdims=True) + RMS_EPS) * (DN_HEAD_DIM ** -0.5)
            k = k * lax.rsqrt(jnp.sum(k * k, -1, keepdims=True) + RMS_EPS)
            beta = beta_all[r0:r0 + c, h:h + 1]
            d_col = dec[:, DN_HEADS + h:DN_HEADS + h + 1]
            d_row = dec_t[DN_HEADS + h:DN_HEADS + h + 1, :]
            d_last = dec[c - 1:c, DN_HEADS + h:DN_HEADS + h + 1]
            diff = d_col - d_row
            lmask = jnp.where(causal, jnp.exp(jnp.where(causal, diff, 0.0)), 0.0)
            kb = k * beta
            vb = v * beta
            k16 = k.astype(BF16)
            a = jnp.where(strict, _dot_nt(kb.astype(BF16), k16) * lmask, 0.0)
            qk = jnp.where(causal, _dot_nt(q.astype(BF16), k16) * lmask, 0.0)
            p = -a
            t = eye + p
            for _ in range(5):
                p = _dot_f32(p, p)
                t = t + _dot_f32(t, p)
            e_col = jnp.exp(d_col)
            u = _dot_f32(t, vb)
            w = _dot_f32(t, kb * e_col)
            s = s_ref[h]
            s16 = s.astype(BF16)
            v_new = u - _dot(w.astype(BF16), s16)
            vn16 = v_new.astype(BF16)
            o = _dot((q * e_col).astype(BF16), s16) + _dot(qk.astype(BF16), vn16)
            k_dec = k * jnp.exp(d_last - d_col)
            s_ref[h] = s * jnp.exp(d_last) + _dot_tn(k_dec.astype(BF16), vn16)
            if o_ref is not None:
                o_n = o * lax.rsqrt(jnp.mean(o * o, -1, keepdims=True) + RMS_EPS) * ng_ref[...]
                o_ref[r0:r0 + c, h * DN_HEAD_DIM:(h + 1) * DN_HEAD_DIM] = o_n


def _gdn_kernel(dn_ref, ba_ref, mdn_ref, mba_ref, cw_ref, alog_ref, dtb_ref, ng_ref, o_ref, xbuf, s_ref):
    @pl.when(pl.program_id(1) == 0)
    def _():
        s_ref[...] = jnp.zeros_like(s_ref)
        xbuf[0:CONV_HIST, :] = jnp.zeros((CONV_HIST, xbuf.shape[1]), F32)
        _gdn_rows(mdn_ref, mba_ref, META_ROWS, META_PAD, None, cw_ref, alog_ref, dtb_ref, ng_ref, xbuf, s_ref)

    _gdn_rows(dn_ref, ba_ref, GDN_BLOCK, 0, o_ref, cw_ref, alog_ref, dtb_ref, ng_ref, xbuf, s_ref)


def _gdn(dn, ba, mdn, mba, conv_w, alog_row, dtb_row, norm_g, *, batch, seq):
    nb = seq // GDN_BLOCK
    const = lambda b, j: (0, 0)
    row = lambda b, j: (b * nb + j, 0)
    return pl.pallas_call(
        _gdn_kernel,
        grid=(batch, nb),
        in_specs=[
            pl.BlockSpec((GDN_BLOCK, 3 * DN_WIDTH), row),
            pl.BlockSpec((GDN_BLOCK, LANES), row),
            pl.BlockSpec((META_ROWS, 3 * DN_WIDTH), const),
            pl.BlockSpec((META_ROWS, LANES), const),
            pl.BlockSpec((DN_CONV, 3 * DN_WIDTH), const),
            pl.BlockSpec((1, LANES), const),
            pl.BlockSpec((1, LANES), const),
            pl.BlockSpec((1, DN_HEAD_DIM), const),
        ],
        out_specs=pl.BlockSpec((GDN_BLOCK, DN_WIDTH), row),
        out_shape=jax.ShapeDtypeStruct((batch * seq, DN_WIDTH), F32),
        scratch_shapes=[
            pltpu.VMEM((GDN_BLOCK + CONV_HIST, 3 * DN_WIDTH), F32),
            pltpu.VMEM((DN_HEADS, DN_HEAD_DIM, DN_HEAD_DIM), F32),
        ],
        compiler_params=pltpu.CompilerParams(dimension_semantics=("arbitrary", "arbitrary"),
                                             vmem_limit_bytes=VMEM_LIMIT),
        name="gdn",
    )(dn, ba, mdn, mba, conv_w, alog_row, dtb_row, norm_g)


SB_TQ = 256
SB_TK = 256
HEADS_PER_BLOCK = LANES // SB_HEAD_DIM


def _neg_strict_upper(n):
    ri = lax.broadcasted_iota(jnp.int32, (n, n), 0)
    ci = lax.broadcasted_iota(jnp.int32, (n, n), 1)
    return jnp.where(ri > ci, -1.0, 0.0).astype(BF16)


def _sb_tile(qh, kt, vt, nu, mask, carry, acc):
    z = _dot_nt(qh, kt)
    sp = _softplus(z)
    if mask is not None:
        sp = jnp.where(mask, sp, 0.0)
    hi = sp.astype(BF16)
    lo = (sp - hi.astype(F32)).astype(BF16)
    later = _dot(hi, nu) + _dot(lo, nu)
    w = jnp.exp(z - sp + later + carry)
    if mask is not None:
        w = jnp.where(mask, w, 0.0)
    acc = acc + _dot(w.astype(BF16), vt)
    carry = carry + later[:, 0:1] - sp[:, 0:1]
    return carry, acc


def _sb_kernel(q_ref, k_ref, v_ref, mk_ref, mv_ref, o_ref):
    i = pl.program_id(2)
    tq, tk = SB_TQ, SB_TK
    q = q_ref[...]
    lane = lax.broadcasted_iota(jnp.int32, q.shape, 1)
    scale = SB_HEAD_DIM ** -0.5
    qs = [jnp.where((lane // SB_HEAD_DIM) == h, q, jnp.zeros_like(q)) * scale for h in range(HEADS_PER_BLOCK)]
    nu = _neg_strict_upper(tk)

    ri = lax.broadcasted_iota(jnp.int32, (tq, tk), 0)
    ci = lax.broadcasted_iota(jnp.int32, (tq, tk), 1)
    diag_mask = ci < ri
    start = pl.multiple_of(i * tk, tk)
    kt = k_ref[pl.ds(start, tk), :]
    vt = v_ref[pl.ds(start, tk), :]
    state = []
    for h in range(HEADS_PER_BLOCK):
        carry, acc = _sb_tile(qs[h], kt, vt, nu, diag_mask, jnp.zeros((tq, 1), F32), jnp.zeros((tq, LANES), F32))
        state += [carry, acc]

    def body(t, st):
        start = pl.multiple_of((i - 1 - t) * tk, tk)
        kt = k_ref[pl.ds(start, tk), :]
        vt = v_ref[pl.ds(start, tk), :]
        out = []
        for h in range(HEADS_PER_BLOCK):
            carry, acc = _sb_tile(qs[h], kt, vt, nu, None, st[2 * h], st[2 * h + 1])
            out += [carry, acc]
        return tuple(out)

    state = lax.fori_loop(0, i, body, tuple(state))

    mk = mk_ref[...]
    mv = mv_ref[...]
    meta_mask = lax.broadcasted_iota(jnp.int32, (tq, META_ROWS), 1) >= META_PAD
    nu_m = _neg_strict_upper(META_ROWS)
    accs = []
    for h in range(HEADS_PER_BLOCK):
        _, acc = _sb_tile(qs[h], mk, mv, nu_m, meta_mask, state[2 * h], state[2 * h + 1])
        accs.append(acc)
    out = accs[0]
    for h in range(1, HEADS_PER_BLOCK):
        out = jnp.where((lane // SB_HEAD_DIM) == h, accs[h], out)
    o_ref[...] = out.astype(o_ref.dtype)


def _sb_attn(sb, msb, *, batch, seq):
    nq = seq // SB_TQ
    n_hb = SB_WIDTH // LANES
    return pl.pallas_call(
        _sb_kernel,
        grid=(batch, n_hb, nq),
        in_specs=[
            pl.BlockSpec((SB_TQ, LANES), lambda b, hp, i: (b * nq + i, hp)),
            pl.BlockSpec((seq, LANES), lambda b, hp, i: (b, n_hb + hp)),
            pl.BlockSpec((seq, LANES), lambda b, hp, i: (b, 2 * n_hb + hp)),
            pl.BlockSpec((META_ROWS, LANES), lambda b, hp, i: (0, n_hb + hp)),
            pl.BlockSpec((META_ROWS, LANES), lambda b, hp, i: (0, 2 * n_hb + hp)),
        ],
        out_specs=pl.BlockSpec((SB_TQ, LANES), lambda b, hp, i: (b * nq + i, hp)),
        out_shape=jax.ShapeDtypeStruct((batch * seq, SB_WIDTH), BF16),
        compiler_params=pltpu.CompilerParams(dimension_semantics=("arbitrary", "arbitrary", "arbitrary"),
                                             vmem_limit_bytes=VMEM_LIMIT),
        name="sb_attn",
    )(sb, sb, sb, msb, msb)


def _masked_lane_max(x, mask):
    return jnp.max(jnp.where(mask, x, -jnp.inf), -1, keepdims=True)


def _first_lane_eq(x, val, mask, lane):
    return jnp.min(jnp.where(mask & (x == val), lane, LANES), -1, keepdims=True)


def _route(logits):
    lane = lax.broadcasted_iota(jnp.int32, logits.shape, 1)
    gmask = lane < N_GROUPS
    gmax = _masked_lane_max(logits, gmask)
    g_idx = _first_lane_eq(logits, gmax, gmask, lane)
    g_prob = 1.0 / jnp.sum(jnp.where(gmask, jnp.exp(logits - gmax), 0.0), -1, keepdims=True)
    lo = ROUTER_COL0 + g_idx * EXPERTS_PER_GROUP
    emask = (lane >= lo) & (lane < lo + EXPERTS_PER_GROUP)
    t1 = _masked_lane_max(logits, emask)
    i1 = _first_lane_eq(logits, t1, emask, lane)
    emask2 = emask & (lane != i1)
    t2 = _masked_lane_max(logits, emask2)
    i2 = _first_lane_eq(logits, t2, emask2, lane)
    e = jnp.exp(t2 - t1)
    w1 = g_prob / (1.0 + e)
    w2 = g_prob * e / (1.0 + e)
    return jnp.where(lane == i1, w1, 0.0) + jnp.where(lane == i2, w2, 0.0)


def _mix_out_kernel(x_ref, odn_ref, osb_ref, g0_ref, b0_ref, wzg_ref, bg_ref, wbdn_ref, wbsb_ref, wout_ref,
                    g1_ref, b1_ref, wr_ref, br_ref, h1_ref, comb_ref):
    h0 = _layer_norm(x_ref[...], g0_ref[...], b0_ref[...])
    zg = _dot(h0.astype(BF16), wzg_ref[...])
    z = zg[:, :DN_WIDTH]
    o_dn = odn_ref[...] * _silu(z)
    a = _dot(o_dn.astype(BF16), wbdn_ref[...])
    b = _dot(osb_ref[...], wbsb_ref[...])
    gate_a = _sigmoid(zg[:, DN_WIDTH:DN_WIDTH + D_MODEL] + bg_ref[0:1, :])
    gate_b = _sigmoid(zg[:, DN_WIDTH + D_MODEL:] + bg_ref[1:2, :])
    merged = gate_a * a + gate_b * b
    mix = _dot(merged.astype(BF16), wout_ref[...])
    h1 = _layer_norm(DEEPNORM_ALPHA * h0 + mix, g1_ref[...], b1_ref[...])
    h1_ref[...] = h1
    logits = _dot(h1.astype(BF16), wr_ref[...]) + br_ref[...]
    comb_ref[...] = _route(logits)


def _mix_out(x2, o_dn, o_sb, g0, b0, wzg, bg, wbdn, wbsb, wout, g1, b1, wr, br, *, tm):
    rows = x2.shape[0]
    const = lambda i: (0, 0)
    row = lambda i: (i, 0)
    full = lambda a: pl.BlockSpec(a.shape, const)
    return pl.pallas_call(
        _mix_out_kernel,
        grid=(rows // tm,),
        in_specs=[
            pl.BlockSpec((tm, D_MODEL), row),
            pl.BlockSpec((tm, DN_WIDTH), row),
            pl.BlockSpec((tm, SB_WIDTH), row),
            full(g0), full(b0), full(wzg), full(bg), full(wbdn), full(wbsb), full(wout), full(g1), full(b1),
            full(wr), full(br),
        ],
        out_specs=[pl.BlockSpec((tm, D_MODEL), row), pl.BlockSpec((tm, LANES), row)],
        out_shape=[jax.ShapeDtypeStruct((rows, D_MODEL), F32), jax.ShapeDtypeStruct((rows, LANES), F32)],
        compiler_params=pltpu.CompilerParams(dimension_semantics=("arbitrary",), vmem_limit_bytes=VMEM_LIMIT),
        name="mix_out",
    )(x2, o_dn, o_sb, g0, b0, wzg, bg, wbdn, wbsb, wout, g1, b1, wr, br)


def _moe_kernel(h1_ref, comb_ref, wg_ref, wu_ref, wd_ref, g2_ref, b2_ref, o_ref, acc_ref):
    g = pl.program_id(1)
    h1 = h1_ref[...]
    hb = h1.astype(BF16)
    hid = _silu(_dot(hb, wg_ref[0])) * _dot(hb, wu_ref[0])
    comb = comb_ref[...]
    lane = lax.broadcasted_iota(jnp.int32, comb.shape, 1)
    parts = []
    for e in range(EXPERTS_PER_GROUP):
        col = ROUTER_COL0 + g * EXPERTS_PER_GROUP + e
        c = jnp.sum(jnp.where(lane == col, comb, 0.0), -1, keepdims=True)
        parts.append(hid[:, e * EXPERT_FF:(e + 1) * EXPERT_FF] * c)
    hid = jnp.concatenate(parts, axis=-1)
    y = _dot(hid.astype(BF16), wd_ref[0])

    @pl.when(g == 0)
    def _():
        acc_ref[...] = y

    @pl.when(g > 0)
    def _():
        acc_ref[...] += y

    @pl.when(g == N_GROUPS - 1)
    def _():
        o_ref[...] = _layer_norm(DEEPNORM_ALPHA * h1 + acc_ref[...], g2_ref[...], b2_ref[...])


def _moe(h1, comb, wg, wu, wd, g2, b2, *, tm):
    rows = h1.shape[0]
    const = lambda i, g: (0, 0)
    row = lambda i, g: (i, 0)
    grp = lambda i, g: (g, 0, 0)
    return pl.pallas_call(
        _moe_kernel,
        grid=(rows // tm, N_GROUPS),
        in_specs=[
            pl.BlockSpec((tm, D_MODEL), row),
            pl.BlockSpec((tm, LANES), row),
            pl.BlockSpec((1,) + wg.shape[1:], grp),
            pl.BlockSpec((1,) + wu.shape[1:], grp),
            pl.BlockSpec((1,) + wd.shape[1:], grp),
            pl.BlockSpec((1, D_MODEL), const),
            pl.BlockSpec((1, D_MODEL), const),
        ],
        out_specs=pl.BlockSpec((tm, D_MODEL), row),
        out_shape=jax.ShapeDtypeStruct((rows, D_MODEL), F32),
        scratch_shapes=[pltpu.VMEM((tm, D_MODEL), F32)],
        compiler_params=pltpu.CompilerParams(dimension_semantics=("arbitrary", "arbitrary"),
                                             vmem_limit_bytes=VMEM_LIMIT),
        name="moe",
    )(h1, comb, wg, wu, wd, g2, b2)


def _pad_lanes(a, col0=0):
    return jnp.pad(a, ((0, 0), (col0, LANES - col0 - a.shape[1])))


def kernel(x, meta_tokens, ln_emb_g, ln_emb_b, w_in, b_gate, dn_conv_w, dn_a_log, dn_dt_bias, dn_norm_g,
           w_branch_dn, w_branch_sb, w_out, ln1_g, ln1_b, router_group_w, router_group_b, router_expert_w,
           router_expert_b, expert_w_gate, expert_w_up, expert_w_down, ln2_g, ln2_b):
    batch, seq, d = x.shape
    assert d == D_MODEL and w_in.shape[0] == 1 and seq % max(GDN_BLOCK, SB_TQ) == 0
    rows = batch * seq
    tm = 512
    assert rows % tm == 0
    row1 = lambda a: a.reshape(1, -1).astype(F32)

    w = w_in[0]
    c0 = 3 * DN_WIDTH
    c1 = c0 + DN_WIDTH
    c2 = c1 + 2 * DN_HEADS
    c3 = c2 + 3 * SB_WIDTH
    w_dn = w[:, :c0].astype(BF16)
    w_ba = _pad_lanes(w[:, c1:c2]).astype(BF16)
    w_sb = w[:, c2:c3].astype(BF16)
    w_zg = jnp.concatenate([w[:, c0:c1], w[:, c3:]], axis=1).astype(BF16)

    x2 = x.reshape(rows, d)
    g0, b0 = row1(ln_emb_g), row1(ln_emb_b)
    dn, sb, ba = _ln_proj(x2, g0, b0, w_dn, w_sb, w_ba, tm=tm)
    xm = jnp.concatenate([jnp.zeros((META_PAD, d), x.dtype), meta_tokens.astype(x.dtype)], axis=0)
    mdn, msb, mba = _ln_proj(xm, g0, b0, w_dn, w_sb, w_ba, tm=META_ROWS, n_zero=META_PAD)

    alog_row = _pad_lanes(dn_a_log[0].reshape(1, -1).astype(F32), DN_HEADS)
    dtb_row = _pad_lanes(dn_dt_bias[0].reshape(1, -1).astype(F32), DN_HEADS)
    o_dn = _gdn(dn, ba, mdn, mba, dn_conv_w[0].astype(F32), alog_row, dtb_row, row1(dn_norm_g[0]),
                batch=batch, seq=seq)
    o_sb = _sb_attn(sb, msb, batch=batch, seq=seq)

    w_r = _pad_lanes(jnp.concatenate(
        [router_group_w[0], router_expert_w[0].transpose(1, 0, 2).reshape(d, N_EXPERTS)], axis=1)).astype(BF16)
    b_r = _pad_lanes(jnp.concatenate(
        [router_group_b[0].reshape(1, -1), router_expert_b[0].reshape(1, -1)], axis=1).astype(F32))
    h1, comb = _mix_out(x2, o_dn, o_sb, g0, b0, w_zg, b_gate[0].astype(F32), w_branch_dn[0].astype(BF16),
                        w_branch_sb[0].astype(BF16), w_out[0].astype(BF16), row1(ln1_g[0]), row1(ln1_b[0]),
                        w_r, b_r, tm=tm)

    ef = EXPERTS_PER_GROUP * EXPERT_FF
    wg = expert_w_gate[0].transpose(0, 2, 1, 3).reshape(N_GROUPS, d, ef).astype(BF16)
    wu = expert_w_up[0].transpose(0, 2, 1, 3).reshape(N_GROUPS, d, ef).astype(BF16)
    wd = expert_w_down[0].reshape(N_GROUPS, ef, d).astype(BF16)
    out = _moe(h1, comb, wg, wu, wd, row1(ln2_g[0]), row1(ln2_b[0]), tm=tm)
    return out.reshape(batch, seq, d)
```

```python
import functools

import jax
import jax.numpy as jnp
import numpy as np
from jax import lax
from jax.experimental import pallas as pl
from jax.experimental.pallas import tpu as pltpu

F32 = jnp.float32
BF16 = jnp.bfloat16

D_MODEL = 1024
N_META = 16
DN_HEADS = 4
DN_HEAD_DIM = 128
DN_WIDTH = DN_HEADS * DN_HEAD_DIM
DN_CONV = 4
DN_CHUNK = 64
SB_HEADS = 8
SB_HEAD_DIM = 64
SB_WIDTH = SB_HEADS * SB_HEAD_DIM
N_GROUPS = 4
EXPERTS_PER_GROUP = 8
N_EXPERTS = N_GROUPS * EXPERTS_PER_GROUP
EXPERT_FF = 256
DEEPNORM_ALPHA = 2.0 ** 0.25
LN_EPS = 1e-5
RMS_EPS = 1e-6

LANES = 128
META_ROWS = DN_CHUNK
META_PAD = META_ROWS - N_META
ROUTER_COL0 = N_GROUPS
VMEM_LIMIT = 56 * 1024 * 1024

HIGHEST = lax.Precision.HIGHEST


def _layer_norm(x, g, b):
    mu = jnp.mean(x, -1, keepdims=True)
    xc = x - mu
    var = jnp.mean(xc * xc, -1, keepdims=True)
    return xc * lax.rsqrt(var + LN_EPS) * g + b


def _sigmoid(x):
    return 1.0 / (1.0 + jnp.exp(-x))


def _softplus(x):
    return jnp.maximum(x, 0.0) + jnp.log(1.0 + jnp.exp(-jnp.abs(x)))


def _silu(x):
    return x * _sigmoid(x)


def _dot(a, b):
    return jnp.dot(a, b, preferred_element_type=F32)


def _dot_nt(a, b):
    return lax.dot_general(a, b, (((1,), (1,)), ((), ())), preferred_element_type=F32)


def _dot_tn(a, b):
    return lax.dot_general(a, b, (((0,), (0,)), ((), ())), preferred_element_type=F32)


def _dot_f32(a, b):
    return jnp.dot(a, b, preferred_element_type=F32, precision=HIGHEST)


def _ln_proj_kernel(x_ref, g_ref, b_ref, wdn_ref, wsb_ref, wba_ref, dn_ref, sb_ref, ba_ref, *, n_zero):
    h = _layer_norm(x_ref[...], g_ref[...], b_ref[...])
    if n_zero:
        rows = lax.broadcasted_iota(jnp.int32, h.shape, 0)
        h = jnp.where(rows >= n_zero, h, 0.0)
    hb = h.astype(BF16)
    dn_ref[...] = _dot(hb, wdn_ref[...])
    sb_ref[...] = _dot(hb, wsb_ref[...]).astype(BF16)
    ba_ref[...] = _dot(hb, wba_ref[...])


def _ln_proj(x2, g, b, wdn, wsb, wba, *, tm, n_zero=0):
    rows = x2.shape[0]
    const = lambda i: (0, 0)
    row = lambda i: (i, 0)
    return pl.pallas_call(
        functools.partial(_ln_proj_kernel, n_zero=n_zero),
        grid=(rows // tm,),
        in_specs=[
            pl.BlockSpec((tm, D_MODEL), row),
            pl.BlockSpec((1, D_MODEL), const),
            pl.BlockSpec((1, D_MODEL), const),
            pl.BlockSpec(wdn.shape, const),
            pl.BlockSpec(wsb.shape, const),
            pl.BlockSpec(wba.shape, const),
        ],
        out_specs=[
            pl.BlockSpec((tm, 3 * DN_WIDTH), row),
            pl.BlockSpec((tm, 3 * SB_WIDTH), row),
            pl.BlockSpec((tm, LANES), row),
        ],
        out_shape=[
            jax.ShapeDtypeStruct((rows, 3 * DN_WIDTH), F32),
            jax.ShapeDtypeStruct((rows, 3 * SB_WIDTH), BF16),
            jax.ShapeDtypeStruct((rows, LANES), F32),
        ],
        compiler_params=pltpu.CompilerParams(dimension_semantics=("arbitrary",), vmem_limit_bytes=VMEM_LIMIT),
        name="ln_proj",
    )(x2, g, b, wdn, wsb, wba)


GDN_BLOCK = 256
CONV_HIST = 8


def _gdn_rows(src_ref, ba_ref, n, n_zero, o_ref, cw_ref, alog_ref, dtb_ref, ng_ref, xbuf, s_ref, vn_ref):
    c = DN_CHUNK
    xbuf[CONV_HIST:CONV_HIST + n, :] = src_ref[...]
    acc = xbuf[CONV_HIST:CONV_HIST + n, :] * cw_ref[DN_CONV - 1:DN_CONV, :]
    for i in range(DN_CONV - 1):
        s = DN_CONV - 1 - i
        acc = acc + xbuf[CONV_HIST - s:CONV_HIST - s + n, :] * cw_ref[i:i + 1, :]
    hist = xbuf[n:n + CONV_HIST, :]
    xbuf[0:CONV_HIST, :] = hist
    qkv = _silu(acc)

    ba = ba_ref[...]
    beta_all = _sigmoid(ba)
    g_all = -jnp.exp(alog_ref[...]) * _softplus(ba + dtb_ref[...])
    if n_zero:
        rows = lax.broadcasted_iota(jnp.int32, g_all.shape, 0)
        g_all = jnp.where(rows >= n_zero, g_all, 0.0)

    ri = lax.broadcasted_iota(jnp.int32, (n, n), 0)
    ci = lax.broadcasted_iota(jnp.int32, (n, n), 1)
    same = (ri // c) == (ci // c)
    causal = same & (ri >= ci)
    strict = same & (ri > ci)
    tril = causal.astype(BF16)
    g_hi = g_all.astype(BF16)
    g_lo = (g_all - g_hi.astype(F32)).astype(BF16)
    dec = _dot(tril, g_hi) + _dot(tril, g_lo)
    dec_t = dec.T

    vn_ref[...] = jnp.zeros_like(vn_ref)
    pre = []
    for h in range(DN_HEADS):
        q = qkv[:, h * DN_HEAD_DIM:(h + 1) * DN_HEAD_DIM]
        k = qkv[:, DN_WIDTH + h * DN_HEAD_DIM:DN_WIDTH + (h + 1) * DN_HEAD_DIM]
        v = qkv[:, 2 * DN_WIDTH + h * DN_HEAD_DIM:2 * DN_WIDTH + (h + 1) * DN_HEAD_DIM]
        q = q * lax.rsqrt(jnp.sum(q * q, -1, keepdims=True) + RMS_EPS) * (DN_HEAD_DIM ** -0.5)
        k = k * lax.rsqrt(jnp.sum(k * k, -1, keepdims=True) + RMS_EPS)
        beta = beta_all[:, h:h + 1]
        d_col = dec[:, DN_HEADS + h:DN_HEADS + h + 1]
        d_row = dec_t[DN_HEADS + h:DN_HEADS + h + 1, :]
        lmask = jnp.where(causal, jnp.exp(jnp.where(causal, d_col - d_row, 0.0)), 0.0)
        kb = k * beta
        vb = v * beta
        k16 = k.astype(BF16)
        a = jnp.where(strict, _dot_nt(kb.astype(BF16), k16) * lmask, 0.0)
        qk = (_dot_nt(q.astype(BF16), k16) * lmask).astype(BF16)
        p = -a
        t = p
        for _ in range(5):
            p16 = p.astype(BF16)
            p = _dot(p16, p16)
            t = t + p + _dot(t.astype(BF16), p.astype(BF16))
        e_col = jnp.exp(d_col)
        rhs = jnp.concatenate([vb, kb * e_col], axis=1)
        uw = rhs + _dot(t.astype(BF16), rhs.astype(BF16))
        pre.append((q * e_col, k, d_col, qk, uw))

    for ch in range(n // c):
        r0 = ch * c
        for h in range(DN_HEADS):
            qe, k, d_col, qk, uw = pre[h]
            d_c = d_col[r0:r0 + c, :]
            d_last = d_col[r0 + c - 1:r0 + c, :]
            s = s_ref[h]
            wq = jnp.concatenate([uw[r0:r0 + c, DN_HEAD_DIM:], qe[r0:r0 + c, :]], axis=0)
            r = _dot(wq.astype(BF16), s.astype(BF16))
            v_new = uw[r0:r0 + c, :DN_HEAD_DIM] - r[:c, :]
            vn16 = v_new.astype(BF16)
            vn_ref[h, r0:r0 + c, :] = vn16
            o = r[c:, :] + _dot(qk[r0:r0 + c, :], vn_ref[h])
            k_dec = k[r0:r0 + c, :] * jnp.exp(d_last - d_c)
            s_ref[h] = s * jnp.exp(d_last) + _dot_tn(k_dec.astype(BF16), vn16)
            if o_ref is not None:
                o_n = o * lax.rsqrt(jnp.mean(o * o, -1, keepdims=True) + RMS_EPS) * ng_ref[...]
                o_ref[r0:r0 + c, h * DN_HEAD_DIM:(h + 1) * DN_HEAD_DIM] = o_n


def _gdn_kernel(dn_ref, ba_ref, mdn_ref, mba_ref, cw_ref, alog_ref, dtb_ref, ng_ref, o_ref, xbuf, s_ref, vn_ref,
                vnm_ref):
    @pl.when(pl.program_id(1) == 0)
    def _():
        s_ref[...] = jnp.zeros_like(s_ref)
        xbuf[0:CONV_HIST, :] = jnp.zeros((CONV_HIST, xbuf.shape[1]), F32)
        _gdn_rows(mdn_ref, mba_ref, META_ROWS, META_PAD, None, cw_ref, alog_ref, dtb_ref, ng_ref, xbuf, s_ref,
                  vnm_ref)

    _gdn_rows(dn_ref, ba_ref, GDN_BLOCK, 0, o_ref, cw_ref, alog_ref, dtb_ref, ng_ref, xbuf, s_ref, vn_ref)


def _gdn(dn, ba, mdn, mba, conv_w, alog_row, dtb_row, norm_g, *, batch, seq):
    nb = seq // GDN_BLOCK
    const = lambda b, j: (0, 0)
    row = lambda b, j: (b * nb + j, 0)
    return pl.pallas_call(
        _gdn_kernel,
        grid=(batch, nb),
        in_specs=[
            pl.BlockSpec((GDN_BLOCK, 3 * DN_WIDTH), row),
            pl.BlockSpec((GDN_BLOCK, LANES), row),
            pl.BlockSpec((META_ROWS, 3 * DN_WIDTH), const),
            pl.BlockSpec((META_ROWS, LANES), const),
            pl.BlockSpec((DN_CONV, 3 * DN_WIDTH), const),
            pl.BlockSpec((1, LANES), const),
            pl.BlockSpec((1, LANES), const),
            pl.BlockSpec((1, DN_HEAD_DIM), const),
        ],
        out_specs=pl.BlockSpec((GDN_BLOCK, DN_WIDTH), row),
        out_shape=jax.ShapeDtypeStruct((batch * seq, DN_WIDTH), F32),
        scratch_shapes=[
            pltpu.VMEM((GDN_BLOCK + CONV_HIST, 3 * DN_WIDTH), F32),
            pltpu.VMEM((DN_HEADS, DN_HEAD_DIM, DN_HEAD_DIM), F32),
            pltpu.VMEM((DN_HEADS, GDN_BLOCK, DN_HEAD_DIM), BF16),
            pltpu.VMEM((DN_HEADS, META_ROWS, DN_HEAD_DIM), BF16),
        ],
        compiler_params=pltpu.CompilerParams(dimension_semantics=("arbitrary", "arbitrary"),
                                             vmem_limit_bytes=VMEM_LIMIT),
        name="gdn",
    )(dn, ba, mdn, mba, conv_w, alog_row, dtb_row, norm_g)


SB_T = 2 * LANES
HEADS_PER_BLOCK = LANES // SB_HEAD_DIM
SB_SLOTS = 4
NEG_BIG = -1e30
LOG2E = 1.4426950408889634


def _sb_schedule(nq):
    qoff, koff, bsel, first = [], [], [], []
    for qi in range(nq):
        for kj in list(range(qi, -1, -1)) + [-1]:
            qoff.append(qi * SB_T)
            koff.append((kj + 1) * SB_T)
            bsel.append(1 if kj == qi else (2 if kj < 0 else 0))
            first.append(1 if kj == qi else 0)
    return [np.asarray(a, np.int32) for a in (qoff, koff, bsel, first)]


def _sb_kernel(qoff_ref, koff_ref, bsel_ref, first_ref, q_ref, k_ref, v_ref, mk_ref, mv_ref, o_ref,
               kbuf, vbuf, qm_s, bias_s, nu_s, z_s, hl_s, later_s, w_s, acc_s, carry_s, *, n_tiles):
    t = SB_T
    seq = k_ref.shape[0]
    for buf, m_ref, x_ref in ((kbuf, mk_ref, k_ref), (vbuf, mv_ref, v_ref)):
        buf[0:t - META_ROWS, :] = jnp.zeros((t - META_ROWS, LANES), BF16)
        buf[t - META_ROWS:t, :] = m_ref[...]
        buf[t:t + seq, :] = x_ref[...]
    ri = lax.broadcasted_iota(jnp.int32, (t, t), 0)
    ci = lax.broadcasted_iota(jnp.int32, (t, t), 1)
    bias_s[0] = jnp.zeros((t, t), F32)
    bias_s[1] = jnp.where(ci < ri, 0.0, NEG_BIG)
    bias_s[2] = jnp.where(ci >= t - N_META, 0.0, NEG_BIG)
    acc_s[...] = jnp.zeros_like(acc_s)
    carry_s[...] = jnp.zeros_like(carry_s)
    rj = lax.broadcasted_iota(jnp.int32, (2 * t, t), 0)
    cs = lax.broadcasted_iota(jnp.int32, (2 * t, t), 1)
    nu_s[...] = jnp.where(jnp.where(rj >= t, rj - t, rj) >= cs, -1.0, 0.0).astype(BF16)
    lane = lax.broadcasted_iota(jnp.int32, (t, LANES), 1)
    scale = SB_HEAD_DIM ** -0.5

    q_all = q_ref[...]
    lane_q = lax.broadcasted_iota(jnp.int32, q_all.shape, 1)
    for h in range(HEADS_PER_BLOCK):
        qm_s[h] = jnp.where((lane_q // SB_HEAD_DIM) == h, q_all, jnp.zeros_like(q_all)) * scale

    def st_logits(n, s):
        qo = pl.multiple_of(qoff_ref[n], t)
        ko = pl.multiple_of(koff_ref[n], t)
        kt = kbuf[pl.ds(ko, t), :]
        bias = bias_s[bsel_ref[n]]
        for h in range(HEADS_PER_BLOCK):
            z_s[s, h] = _dot_nt(qm_s[h, pl.ds(qo, t), :], kt) + bias

    def st_softplus(n, s):
        for h in range(HEADS_PER_BLOCK):
            z = z_s[s, h]
            sp = jnp.maximum(z, 0.0) + jnp.log(1.0 + jnp.exp(-jnp.abs(z)))
            hi = sp.astype(BF16)
            hl_s[s, h, :, 0:t] = hi
            hl_s[s, h, :, t:2 * t] = (sp - hi.astype(F32)).astype(BF16)

    def st_keysum(n, s):
        for h in range(HEADS_PER_BLOCK):
            later_s[s, h] = _dot(hl_s[s, h], nu_s[...])

    def st_weights(n, s):
        keep = jnp.where(first_ref[n] == 1, 0.0, 1.0)
        for h in range(HEADS_PER_BLOCK):
            later = later_s[s, h]
            carry = carry_s[h] * keep
            logw = z_s[s, h] + later + jnp.concatenate([carry] * (t // LANES), axis=1)
            w_s[s, h] = jnp.exp(logw).astype(BF16)
            carry_s[h] = carry + jnp.broadcast_to(later[:, 0:1], (t, LANES))

    def st_values(n, s):
        qo = pl.multiple_of(qoff_ref[n], t)
        ko = pl.multiple_of(koff_ref[n], t)
        vt = vbuf[pl.ds(ko, t), :]
        keep = jnp.where(first_ref[n] == 1, 0.0, 1.0)
        accs = []
        for h in range(HEADS_PER_BLOCK):
            acc = acc_s[h] * keep + _dot(w_s[s, h], vt)
            acc_s[h] = acc
            accs.append(acc)
        out = accs[0]
        for h in range(1, HEADS_PER_BLOCK):
            out = jnp.where((lane // SB_HEAD_DIM) == h, accs[h], out)
        o_ref[pl.ds(qo, t), :] = out.astype(o_ref.dtype)

    stages = (st_logits, st_softplus, st_keysum, st_weights, st_values)
    depth = len(stages)

    def trip(it, parity):
        for d in reversed(range(depth)):
            n = it - d
            if isinstance(n, int) and not 0 <= n < n_tiles:
                continue
            stages[d](n, (parity - d) % SB_SLOTS)

    for it in range(depth - 1):
        trip(it, it % SB_SLOTS)
    first_full = depth - 1
    n_full = n_tiles - first_full
    n_loop = n_full // SB_SLOTS

    def body(u, c):
        base = first_full + u * SB_SLOTS
        for j in range(SB_SLOTS):
            trip(base + j, (first_full + j) % SB_SLOTS)
        return c

    lax.fori_loop(0, n_loop, body, 0)
    for it in range(first_full + n_loop * SB_SLOTS, n_tiles + depth - 1):
        trip(it, it % SB_SLOTS)


def _sb_attn(sb, msb, *, batch, seq):
    nq = seq // SB_T
    n_hb = SB_WIDTH // LANES
    sched = _sb_schedule(nq)
    n_tiles = int(sched[0].shape[0])
    grid_spec = pltpu.PrefetchScalarGridSpec(
        num_scalar_prefetch=len(sched),
        grid=(batch, n_hb),
        in_specs=[
            pl.BlockSpec((seq, LANES), lambda b, hp, *_: (b, hp)),
            pl.BlockSpec((seq, LANES), lambda b, hp, *_: (b, n_hb + hp)),
            pl.BlockSpec((seq, LANES), lambda b, hp, *_: (b, 2 * n_hb + hp)),
            pl.BlockSpec((META_ROWS, LANES), lambda b, hp, *_: (0, n_hb + hp)),
            pl.BlockSpec((META_ROWS, LANES), lambda b, hp, *_: (0, 2 * n_hb + hp)),
        ],
        out_specs=pl.BlockSpec((seq, LANES), lambda b, hp, *_: (b, hp)),
        scratch_shapes=[
            pltpu.VMEM((SB_T + seq, LANES), BF16),
            pltpu.VMEM((SB_T + seq, LANES), BF16),
            pltpu.VMEM((HEADS_PER_BLOCK, seq, LANES), BF16),
            pltpu.VMEM((3, SB_T, SB_T), F32),
            pltpu.VMEM((2 * SB_T, SB_T), BF16),
            pltpu.VMEM((SB_SLOTS, HEADS_PER_BLOCK, SB_T, SB_T), F32),
            pltpu.VMEM((SB_SLOTS, HEADS_PER_BLOCK, SB_T, 2 * SB_T), BF16),
            pltpu.VMEM((SB_SLOTS, HEADS_PER_BLOCK, SB_T, SB_T), F32),
            pltpu.VMEM((SB_SLOTS, HEADS_PER_BLOCK, SB_T, SB_T), BF16),
            pltpu.VMEM((HEADS_PER_BLOCK, SB_T, LANES), F32),
            pltpu.VMEM((HEADS_PER_BLOCK, SB_T, LANES), F32),
        ],
    )
    return pl.pallas_call(
        functools.partial(_sb_kernel, n_tiles=n_tiles),
        grid_spec=grid_spec,
        out_shape=jax.ShapeDtypeStruct((batch * seq, SB_WIDTH), BF16),
        compiler_params=pltpu.CompilerParams(dimension_semantics=("arbitrary", "arbitrary"),
                                             vmem_limit_bytes=VMEM_LIMIT),
        name="sb_attn",
    )(*[jnp.asarray(a) for a in sched], sb, sb, sb, msb, msb)


def _masked_lane_max(x, mask):
    return jnp.max(jnp.where(mask, x, -jnp.inf), -1, keepdims=True)


def _first_lane_eq(x, val, mask, lane):
    return jnp.min(jnp.where(mask & (x == val), lane, LANES), -1, keepdims=True)


def _route(logits):
    lane = lax.broadcasted_iota(jnp.int32, logits.shape, 1)
    gmask = lane < N_GROUPS
    gmax = _masked_lane_max(logits, gmask)
    g_idx = _first_lane_eq(logits, gmax, gmask, lane)
    g_prob = 1.0 / jnp.sum(jnp.where(gmask, jnp.exp(logits - gmax), 0.0), -1, keepdims=True)
    lo = ROUTER_COL0 + g_idx * EXPERTS_PER_GROUP
    emask = (lane >= lo) & (lane < lo + EXPERTS_PER_GROUP)
    t1 = _masked_lane_max(logits, emask)
    i1 = _first_lane_eq(logits, t1, emask, lane)
    emask2 = emask & (lane != i1)
    t2 = _masked_lane_max(logits, emask2)
    i2 = _first_lane_eq(logits, t2, emask2, lane)
    e = jnp.exp(t2 - t1)
    w1 = g_prob / (1.0 + e)
    w2 = g_prob * e / (1.0 + e)
    return jnp.where(lane == i1, w1, 0.0) + jnp.where(lane == i2, w2, 0.0)


def _mix_out_kernel(x_ref, odn_ref, osb_ref, g0_ref, b0_ref, wzg_ref, bg_ref, wbdn_ref, wbsb_ref, wout_ref,
                    g1_ref, b1_ref, wr_ref, br_ref, h1_ref, comb_ref):
    h0 = _layer_norm(x_ref[...], g0_ref[...], b0_ref[...])
    zg = _dot(h0.astype(BF16), wzg_ref[...])
    z = zg[:, :DN_WIDTH]
    o_dn = odn_ref[...] * _silu(z)
    a = _dot(o_dn.astype(BF16), wbdn_ref[...])
    b = _dot(osb_ref[...], wbsb_ref[...])
    gate_a = _sigmoid(zg[:, DN_WIDTH:DN_WIDTH + D_MODEL] + bg_ref[0:1, :])
    gate_b = _sigmoid(zg[:, DN_WIDTH + D_MODEL:] + bg_ref[1:2, :])
    merged = gate_a * a + gate_b * b
    mix = _dot(merged.astype(BF16), wout_ref[...])
    h1 = _layer_norm(DEEPNORM_ALPHA * h0 + mix, g1_ref[...], b1_ref[...])
    h1_ref[...] = h1
    logits = _dot(h1.astype(BF16), wr_ref[...]) + br_ref[...]
    comb_ref[...] = _route(logits)


def _mix_out(x2, o_dn, o_sb, g0, b0, wzg, bg, wbdn, wbsb, wout, g1, b1, wr, br, *, tm):
    rows = x2.shape[0]
    const = lambda i: (0, 0)
    row = lambda i: (i, 0)
    full = lambda a: pl.BlockSpec(a.shape, const)
    return pl.pallas_call(
        _mix_out_kernel,
        grid=(rows // tm,),
        in_specs=[
            pl.BlockSpec((tm, D_MODEL), row),
            pl.BlockSpec((tm, DN_WIDTH), row),
            pl.BlockSpec((tm, SB_WIDTH), row),
            full(g0), full(b0), full(wzg), full(bg), full(wbdn), full(wbsb), full(wout), full(g1), full(b1),
            full(wr), full(br),
        ],
        out_specs=[pl.BlockSpec((tm, D_MODEL), row), pl.BlockSpec((tm, LANES), row)],
        out_shape=[jax.ShapeDtypeStruct((rows, D_MODEL), F32), jax.ShapeDtypeStruct((rows, LANES), F32)],
        compiler_params=pltpu.CompilerParams(dimension_semantics=("arbitrary",), vmem_limit_bytes=VMEM_LIMIT),
        name="mix_out",
    )(x2, o_dn, o_sb, g0, b0, wzg, bg, wbdn, wbsb, wout, g1, b1, wr, br)


def _moe_kernel(h1_ref, comb_ref, wg_ref, wu_ref, wd_ref, g2_ref, b2_ref, o_ref, acc_ref):
    g = pl.program_id(1)
    h1 = h1_ref[...]
    hb = h1.astype(BF16)
    hid = _silu(_dot(hb, wg_ref[0])) * _dot(hb, wu_ref[0])
    comb = comb_ref[...]
    lane = lax.broadcasted_iota(jnp.int32, comb.shape, 1)
    parts = []
    for e in range(EXPERTS_PER_GROUP):
        col = ROUTER_COL0 + g * EXPERTS_PER_GROUP + e
        c = jnp.sum(jnp.where(lane == col, comb, 0.0), -1, keepdims=True)
        parts.append(hid[:, e * EXPERT_FF:(e + 1) * EXPERT_FF] * c)
    hid = jnp.concatenate(parts, axis=-1)
    y = _dot(hid.astype(BF16), wd_ref[0])

    @pl.when(g == 0)
    def _():
        acc_ref[...] = y

    @pl.when(g > 0)
    def _():
        acc_ref[...] += y

    @pl.when(g == N_GROUPS - 1)
    def _():
        o_ref[...] = _layer_norm(DEEPNORM_ALPHA * h1 + acc_ref[...], g2_ref[...], b2_ref[...])


def _moe(h1, comb, wg, wu, wd, g2, b2, *, tm):
    rows = h1.shape[0]
    const = lambda i, g: (0, 0)
    row = lambda i, g: (i, 0)
    grp = lambda i, g: (g, 0, 0)
    return pl.pallas_call(
        _moe_kernel,
        grid=(rows // tm, N_GROUPS),
        in_specs=[
            pl.BlockSpec((tm, D_MODEL), row),
            pl.BlockSpec((tm, LANES), row),
            pl.BlockSpec((1,) + wg.shape[1:], grp),
            pl.BlockSpec((1,) + wu.shape[1:], grp),
            pl.BlockSpec((1,) + wd.shape[1:], grp),
            pl.BlockSpec((1, D_MODEL), const),
            pl.BlockSpec((1, D_MODEL), const),
        ],
        out_specs=pl.BlockSpec((tm, D_MODEL), row),
        out_shape=jax.ShapeDtypeStruct((rows, D_MODEL), F32),
        scratch_shapes=[pltpu.VMEM((tm, D_MODEL), F32)],
        compiler_params=pltpu.CompilerParams(dimension_semantics=("arbitrary", "arbitrary"),
                                             vmem_limit_bytes=VMEM_LIMIT),
        name="moe",
    )(h1, comb, wg, wu, wd, g2, b2)


def _pad_lanes(a, col0=0):
    return jnp.pad(a, ((0, 0), (col0, LANES - col0 - a.shape[1])))


def kernel(x, meta_tokens, ln_emb_g, ln_emb_b, w_in, b_gate, dn_conv_w, dn_a_log, dn_dt_bias, dn_norm_g,
           w_branch_dn, w_branch_sb, w_out, ln1_g, ln1_b, router_group_w, router_group_b, router_expert_w,
           router_expert_b, expert_w_gate, expert_w_up, expert_w_down, ln2_g, ln2_b):
    batch, seq, d = x.shape
    assert d == D_MODEL and w_in.shape[0] == 1 and seq % max(GDN_BLOCK, SB_T) == 0
    rows = batch * seq
    tm = 512
    assert rows % tm == 0
    row1 = lambda a: a.reshape(1, -1).astype(F32)

    w = w_in[0]
    c0 = 3 * DN_WIDTH
    c1 = c0 + DN_WIDTH
    c2 = c1 + 2 * DN_HEADS
    c3 = c2 + 3 * SB_WIDTH
    w_dn = w[:, :c0].astype(BF16)
    w_ba = _pad_lanes(w[:, c1:c2]).astype(BF16)
    w_sb = w[:, c2:c3].astype(BF16)
    w_zg = jnp.concatenate([w[:, c0:c1], w[:, c3:]], axis=1).astype(BF16)

    x2 = x.reshape(rows, d)
    g0, b0 = row1(ln_emb_g), row1(ln_emb_b)
    dn, sb, ba = _ln_proj(x2, g0, b0, w_dn, w_sb, w_ba, tm=tm)
    xm = jnp.concatenate([jnp.zeros((META_PAD, d), x.dtype), meta_tokens.astype(x.dtype)], axis=0)
    mdn, msb, mba = _ln_proj(xm, g0, b0, w_dn, w_sb, w_ba, tm=META_ROWS, n_zero=META_PAD)

    alog_row = _pad_lanes(dn_a_log[0].reshape(1, -1).astype(F32), DN_HEADS)
    dtb_row = _pad_lanes(dn_dt_bias[0].reshape(1, -1).astype(F32), DN_HEADS)
    o_dn = _gdn(dn, ba, mdn, mba, dn_conv_w[0].astype(F32), alog_row, dtb_row, row1(dn_norm_g[0]),
                batch=batch, seq=seq)
    o_sb = _sb_attn(sb, msb, batch=batch, seq=seq)

    w_r = _pad_lanes(jnp.concatenate(
        [router_group_w[0], router_expert_w[0].transpose(1, 0, 2).reshape(d, N_EXPERTS)], axis=1)).astype(BF16)
    b_r = _pad_lanes(jnp.concatenate(
        [router_group_b[0].reshape(1, -1), router_expert_b[0].reshape(1, -1)], axis=1).astype(F32))
    h1, comb = _mix_out(x2, o_dn, o_sb, g0, b0, w_zg, b_gate[0].astype(F32), w_branch_dn[0].astype(BF16),
                        w_branch_sb[0].astype(BF16), w_out[0].astype(BF16), row1(ln1_g[0]), row1(ln1_b[0]),
                        w_r, b_r, tm=tm)

    ef = EXPERTS_PER_GROUP * EXPERT_FF
    wg = expert_w_gate[0].transpose(0, 2, 1, 3).reshape(N_GROUPS, d, ef).astype(BF16)
    wu = expert_w_up[0].transpose(0, 2, 1, 3).reshape(N_GROUPS, d, ef).astype(BF16)
    wd = expert_w_down[0].reshape(N_GROUPS, ef, d).astype(BF16)
    out = _moe(h1, comb, wg, wu, wd, row1(ln2_g[0]), row1(ln2_b[0]), tm=tm)
    return out.reshape(batch, seq, d)
```

```python
import functools

import jax
import jax.numpy as jnp
import numpy as np
from jax import lax
from jax.experimental import pallas as pl
from jax.experimental.pallas import tpu as pltpu

F32 = jnp.float32
BF16 = jnp.bfloat16

D_MODEL = 1024
N_META = 16
DN_HEADS = 4
DN_HEAD_DIM = 128
DN_WIDTH = DN_HEADS * DN_HEAD_DIM
DN_CONV = 4
DN_CHUNK = 64
SB_HEADS = 8
SB_HEAD_DIM = 64
SB_WIDTH = SB_HEADS * SB_HEAD_DIM
N_GROUPS = 4
EXPERTS_PER_GROUP = 8
N_EXPERTS = N_GROUPS * EXPERTS_PER_GROUP
EXPERT_FF = 256
DEEPNORM_ALPHA = 2.0 ** 0.25
LN_EPS = 1e-5
RMS_EPS = 1e-6

LANES = 128
META_ROWS = DN_CHUNK
META_PAD = META_ROWS - N_META
ROUTER_COL0 = N_GROUPS
VMEM_LIMIT = 56 * 1024 * 1024

HIGHEST = lax.Precision.HIGHEST


def _layer_norm(x, g, b):
    mu = jnp.mean(x, -1, keepdims=True)
    xc = x - mu
    var = jnp.mean(xc * xc, -1, keepdims=True)
    return xc * lax.rsqrt(var + LN_EPS) * g + b


def _sigmoid(x):
    return 1.0 / (1.0 + jnp.exp(-x))


def _softplus(x):
    return jnp.maximum(x, 0.0) + jnp.log(1.0 + jnp.exp(-jnp.abs(x)))


def _silu(x):
    return x * _sigmoid(x)


def _dot(a, b):
    return jnp.dot(a, b, preferred_element_type=F32)


def _dot_nt(a, b):
    return lax.dot_general(a, b, (((1,), (1,)), ((), ())), preferred_element_type=F32)


def _dot_tn(a, b):
    return lax.dot_general(a, b, (((0,), (0,)), ((), ())), preferred_element_type=F32)


def _dot_f32(a, b):
    return jnp.dot(a, b, preferred_element_type=F32, precision=HIGHEST)


def _ln_proj_kernel(x_ref, g_ref, b_ref, wdn_ref, wsb_ref, wba_ref, dn_ref, sb_ref, ba_ref, *, n_zero):
    h = _layer_norm(x_ref[...], g_ref[...], b_ref[...])
    if n_zero:
        rows = lax.broadcasted_iota(jnp.int32, h.shape, 0)
        h = jnp.where(rows >= n_zero, h, 0.0)
    hb = h.astype(BF16)
    dn_ref[...] = _dot(hb, wdn_ref[...])
    sb_ref[...] = _dot(hb, wsb_ref[...]).astype(BF16)
    ba_ref[...] = _dot(hb, wba_ref[...])


def _ln_proj(x2, g, b, wdn, wsb, wba, *, tm, n_zero=0):
    rows = x2.shape[0]
    const = lambda i: (0, 0)
    row = lambda i: (i, 0)
    return pl.pallas_call(
        functools.partial(_ln_proj_kernel, n_zero=n_zero),
        grid=(rows // tm,),
        in_specs=[
            pl.BlockSpec((tm, D_MODEL), row),
            pl.BlockSpec((1, D_MODEL), const),
            pl.BlockSpec((1, D_MODEL), const),
            pl.BlockSpec(wdn.shape, const),
            pl.BlockSpec(wsb.shape, const),
            pl.BlockSpec(wba.shape, const),
        ],
        out_specs=[
            pl.BlockSpec((tm, 3 * DN_WIDTH), row),
            pl.BlockSpec((tm, 3 * SB_WIDTH), row),
            pl.BlockSpec((tm, LANES), row),
        ],
        out_shape=[
            jax.ShapeDtypeStruct((rows, 3 * DN_WIDTH), F32),
            jax.ShapeDtypeStruct((rows, 3 * SB_WIDTH), BF16),
            jax.ShapeDtypeStruct((rows, LANES), F32),
        ],
        compiler_params=pltpu.CompilerParams(dimension_semantics=("arbitrary",), vmem_limit_bytes=VMEM_LIMIT),
        name="ln_proj",
    )(x2, g, b, wdn, wsb, wba)


GDN_BLOCK = 256
CONV_HIST = 8


def _gdn_rows(src_ref, ba_ref, n, n_zero, o_ref, cw_ref, alog_ref, dtb_ref, ng_ref, xbuf, s_ref, vn_ref):
    c = DN_CHUNK
    xbuf[CONV_HIST:CONV_HIST + n, :] = src_ref[...]
    acc = xbuf[CONV_HIST:CONV_HIST + n, :] * cw_ref[DN_CONV - 1:DN_CONV, :]
    for i in range(DN_CONV - 1):
        s = DN_CONV - 1 - i
        acc = acc + xbuf[CONV_HIST - s:CONV_HIST - s + n, :] * cw_ref[i:i + 1, :]
    hist = xbuf[n:n + CONV_HIST, :]
    xbuf[0:CONV_HIST, :] = hist
    qkv = _silu(acc)

    ba = ba_ref[...]
    beta_all = _sigmoid(ba)
    g_all = -jnp.exp(alog_ref[...]) * _softplus(ba + dtb_ref[...])
    if n_zero:
        rows = lax.broadcasted_iota(jnp.int32, g_all.shape, 0)
        g_all = jnp.where(rows >= n_zero, g_all, 0.0)

    ri = lax.broadcasted_iota(jnp.int32, (n, n), 0)
    ci = lax.broadcasted_iota(jnp.int32, (n, n), 1)
    same = (ri // c) == (ci // c)
    causal = same & (ri >= ci)
    strict = same & (ri > ci)
    tril = causal.astype(BF16)
    g_hi = g_all.astype(BF16)
    g_lo = (g_all - g_hi.astype(F32)).astype(BF16)
    dec = _dot(tril, g_hi) + _dot(tril, g_lo)
    dec_t = dec.T

    vn_ref[...] = jnp.zeros_like(vn_ref)
    heads = range(DN_HEADS)
    qe, ks, d_cols, qks, rhss, ps = [], [], [], [], [], []
    for h in heads:
        q = qkv[:, h * DN_HEAD_DIM:(h + 1) * DN_HEAD_DIM]
        k = qkv[:, DN_WIDTH + h * DN_HEAD_DIM:DN_WIDTH + (h + 1) * DN_HEAD_DIM]
        v = qkv[:, 2 * DN_WIDTH + h * DN_HEAD_DIM:2 * DN_WIDTH + (h + 1) * DN_HEAD_DIM]
        q = q * lax.rsqrt(jnp.sum(q * q, -1, keepdims=True) + RMS_EPS) * (DN_HEAD_DIM ** -0.5)
        k = k * lax.rsqrt(jnp.sum(k * k, -1, keepdims=True) + RMS_EPS)
        beta = beta_all[:, h:h + 1]
        d_col = dec[:, DN_HEADS + h:DN_HEADS + h + 1]
        d_row = dec_t[DN_HEADS + h:DN_HEADS + h + 1, :]
        lmask = jnp.where(causal, jnp.exp(jnp.where(causal, d_col - d_row, 0.0)), 0.0)
        kb = k * beta
        k16 = k.astype(BF16)
        a = jnp.where(strict, _dot_nt(kb.astype(BF16), k16) * lmask, 0.0)
        qks.append((_dot_nt(q.astype(BF16), k16) * lmask).astype(BF16))
        e_col = jnp.exp(d_col)
        rhss.append(jnp.concatenate([v * beta, kb * e_col], axis=1))
        qe.append(q * e_col)
        ks.append(k)
        d_cols.append(d_col)
        ps.append(-a)
    ts = list(ps)
    for _ in range(5):
        p16s = [p.astype(BF16) for p in ps]
        ps = [_dot(p16, p16) for p16 in p16s]
        ts = [t + p + _dot(t.astype(BF16), p.astype(BF16)) for t, p in zip(ts, ps)]
    uws = [rhs + _dot(t.astype(BF16), rhs.astype(BF16)) for t, rhs in zip(ts, rhss)]

    for ch in range(n // c):
        r0 = ch * c
        for h in heads:
            k, d_col, qk, uw = ks[h], d_cols[h], qks[h], uws[h]
            d_c = d_col[r0:r0 + c, :]
            d_last = d_col[r0 + c - 1:r0 + c, :]
            s = s_ref[h]
            wq = jnp.concatenate([uw[r0:r0 + c, DN_HEAD_DIM:], qe[h][r0:r0 + c, :]], axis=0)
            r = _dot(wq.astype(BF16), s.astype(BF16))
            v_new = uw[r0:r0 + c, :DN_HEAD_DIM] - r[:c, :]
            vn16 = v_new.astype(BF16)
            vn_ref[h, r0:r0 + c, :] = vn16
            o = r[c:, :] + _dot(qk[r0:r0 + c, :], vn_ref[h])
            k_dec = k[r0:r0 + c, :] * jnp.exp(d_last - d_c)
            s_ref[h] = s * jnp.exp(d_last) + _dot_tn(k_dec.astype(BF16), vn16)
            if o_ref is not None:
                o_n = o * lax.rsqrt(jnp.mean(o * o, -1, keepdims=True) + RMS_EPS) * ng_ref[...]
                o_ref[r0:r0 + c, h * DN_HEAD_DIM:(h + 1) * DN_HEAD_DIM] = o_n


def _gdn_kernel(dn_ref, ba_ref, mdn_ref, mba_ref, cw_ref, alog_ref, dtb_ref, ng_ref, o_ref, xbuf, s_ref, vn_ref,
                vnm_ref):
    @pl.when(pl.program_id(1) == 0)
    def _():
        s_ref[...] = jnp.zeros_like(s_ref)
        xbuf[0:CONV_HIST, :] = jnp.zeros((CONV_HIST, xbuf.shape[1]), F32)
        _gdn_rows(mdn_ref, mba_ref, META_ROWS, META_PAD, None, cw_ref, alog_ref, dtb_ref, ng_ref, xbuf, s_ref,
                  vnm_ref)

    _gdn_rows(dn_ref, ba_ref, GDN_BLOCK, 0, o_ref, cw_ref, alog_ref, dtb_ref, ng_ref, xbuf, s_ref, vn_ref)


def _gdn(dn, ba, mdn, mba, conv_w, alog_row, dtb_row, norm_g, *, batch, seq):
    nb = seq // GDN_BLOCK
    const = lambda b, j: (0, 0)
    row = lambda b, j: (b * nb + j, 0)
    return pl.pallas_call(
        _gdn_kernel,
        grid=(batch, nb),
        in_specs=[
            pl.BlockSpec((GDN_BLOCK, 3 * DN_WIDTH), row),
            pl.BlockSpec((GDN_BLOCK, LANES), row),
            pl.BlockSpec((META_ROWS, 3 * DN_WIDTH), const),
            pl.BlockSpec((META_ROWS, LANES), const),
            pl.BlockSpec((DN_CONV, 3 * DN_WIDTH), const),
            pl.BlockSpec((1, LANES), const),
            pl.BlockSpec((1, LANES), const),
            pl.BlockSpec((1, DN_HEAD_DIM), const),
        ],
        out_specs=pl.BlockSpec((GDN_BLOCK, DN_WIDTH), row),
        out_shape=jax.ShapeDtypeStruct((batch * seq, DN_WIDTH), F32),
        scratch_shapes=[
            pltpu.VMEM((GDN_BLOCK + CONV_HIST, 3 * DN_WIDTH), F32),
            pltpu.VMEM((DN_HEADS, DN_HEAD_DIM, DN_HEAD_DIM), F32),
            pltpu.VMEM((DN_HEADS, GDN_BLOCK, DN_HEAD_DIM), BF16),
            pltpu.VMEM((DN_HEADS, META_ROWS, DN_HEAD_DIM), BF16),
        ],
        compiler_params=pltpu.CompilerParams(dimension_semantics=("arbitrary", "arbitrary"),
                                             vmem_limit_bytes=VMEM_LIMIT),
        name="gdn",
    )(dn, ba, mdn, mba, conv_w, alog_row, dtb_row, norm_g)


SB_T = 2 * LANES
HEADS_PER_BLOCK = LANES // SB_HEAD_DIM
SB_SLOTS = 4
SB_UNROLL = 8
NEG_BIG = -1e30
LOG2E = 1.4426950408889634


def _sb_schedule(nq):
    qoff, koff, bsel, first = [], [], [], []
    for qi in range(nq):
        for kj in list(range(qi, -1, -1)) + [-1]:
            qoff.append(qi * SB_T)
            koff.append((kj + 1) * SB_T)
            bsel.append(1 if kj == qi else (2 if kj < 0 else 0))
            first.append(1 if kj == qi else 0)
    return [np.asarray(a, np.int32) for a in (qoff, koff, bsel, first)]


def _sb_kernel(qoff_ref, koff_ref, bsel_ref, first_ref, q_ref, k_ref, v_ref, mk_ref, mv_ref, o_ref,
               kbuf, vbuf, qm_s, bias_s, nu_s, z_s, sp_s, later_s, w_s, acc_s, carry_s, *, n_tiles):
    t = SB_T
    seq = k_ref.shape[0]
    for buf, m_ref, x_ref in ((kbuf, mk_ref, k_ref), (vbuf, mv_ref, v_ref)):
        buf[0:t - META_ROWS, :] = jnp.zeros((t - META_ROWS, LANES), BF16)
        buf[t - META_ROWS:t, :] = m_ref[...]
        buf[t:t + seq, :] = x_ref[...]
    ri = lax.broadcasted_iota(jnp.int32, (t, t), 0)
    ci = lax.broadcasted_iota(jnp.int32, (t, t), 1)
    bias_s[0] = jnp.zeros((t, t), F32)
    bias_s[1] = jnp.where(ci < ri, 0.0, NEG_BIG)
    bias_s[2] = jnp.where(ci >= t - N_META, 0.0, NEG_BIG)
    acc_s[...] = jnp.zeros_like(acc_s)
    carry_s[...] = jnp.zeros_like(carry_s)
    nu_s[...] = jnp.where(ri > ci, -1.0, 0.0).astype(BF16)
    lane = lax.broadcasted_iota(jnp.int32, (t, LANES), 1)
    scale = SB_HEAD_DIM ** -0.5

    q_all = q_ref[...]
    lane_q = lax.broadcasted_iota(jnp.int32, q_all.shape, 1)
    for h in range(HEADS_PER_BLOCK):
        qm_s[h] = jnp.where((lane_q // SB_HEAD_DIM) == h, q_all, jnp.zeros_like(q_all)) * scale

    def st_logits(n, s):
        qo = pl.multiple_of(qoff_ref[n], t)
        ko = pl.multiple_of(koff_ref[n], t)
        kt = kbuf[pl.ds(ko, t), :]
        bias = bias_s[bsel_ref[n]]
        for h in range(HEADS_PER_BLOCK):
            z_s[s, h] = _dot_nt(qm_s[h, pl.ds(qo, t), :], kt) + bias

    def st_softplus(n, s):
        for h in range(HEADS_PER_BLOCK):
            z = z_s[s, h]
            sp = jnp.maximum(z, 0.0) + jnp.log(1.0 + jnp.exp(-jnp.abs(z)))
            sp_s[s, h] = sp.astype(BF16)
            z_s[s, h] = z - sp

    def st_keysum(n, s):
        for h in range(HEADS_PER_BLOCK):
            later_s[s, h] = _dot(sp_s[s, h], nu_s[...])

    def st_weights(n, s):
        keep = jnp.where(first_ref[n] == 1, 0.0, 1.0)
        for h in range(HEADS_PER_BLOCK):
            later = later_s[s, h]
            carry = carry_s[h] * keep
            logw = z_s[s, h] + later + jnp.concatenate([carry] * (t // LANES), axis=1)
            w_s[s, h] = jnp.exp(logw).astype(BF16)
            total = later[:, 0:1] - sp_s[s, h, :, 0:1].astype(F32)
            carry_s[h] = carry + jnp.broadcast_to(total, (t, LANES))

    def st_values(n, s):
        qo = pl.multiple_of(qoff_ref[n], t)
        ko = pl.multiple_of(koff_ref[n], t)
        vt = vbuf[pl.ds(ko, t), :]
        keep = jnp.where(first_ref[n] == 1, 0.0, 1.0)
        accs = []
        for h in range(HEADS_PER_BLOCK):
            acc = acc_s[h] * keep + _dot(w_s[s, h], vt)
            acc_s[h] = acc
            accs.append(acc)
        out = accs[0]
        for h in range(1, HEADS_PER_BLOCK):
            out = jnp.where((lane // SB_HEAD_DIM) == h, accs[h], out)
        o_ref[pl.ds(qo, t), :] = out.astype(o_ref.dtype)

    stages = (st_logits, st_softplus, st_keysum, st_weights, st_values)
    depth = len(stages)

    def trip(it, parity):
        for d in reversed(range(depth)):
            n = it - d
            if isinstance(n, int) and not 0 <= n < n_tiles:
                continue
            stages[d](n, (parity - d) % SB_SLOTS)

    for it in range(depth - 1):
        trip(it, it % SB_SLOTS)
    first_full = depth - 1
    n_full = n_tiles - first_full
    n_loop = n_full // SB_UNROLL

    def body(u, c):
        base = first_full + u * SB_UNROLL
        for j in range(SB_UNROLL):
            trip(base + j, (first_full + j) % SB_SLOTS)
        return c

    lax.fori_loop(0, n_loop, body, 0)
    for it in range(first_full + n_loop * SB_UNROLL, n_tiles + depth - 1):
        trip(it, it % SB_SLOTS)


def _sb_attn(sb, msb, *, batch, seq):
    nq = seq // SB_T
    n_hb = SB_WIDTH // LANES
    sched = _sb_schedule(nq)
    n_tiles = int(sched[0].shape[0])
    grid_spec = pltpu.PrefetchScalarGridSpec(
        num_scalar_prefetch=len(sched),
        grid=(batch, n_hb),
        in_specs=[
            pl.BlockSpec((seq, LANES), lambda b, hp, *_: (b, hp)),
            pl.BlockSpec((seq, LANES), lambda b, hp, *_: (b, n_hb + hp)),
            pl.BlockSpec((seq, LANES), lambda b, hp, *_: (b, 2 * n_hb + hp)),
            pl.BlockSpec((META_ROWS, LANES), lambda b, hp, *_: (0, n_hb + hp)),
            pl.BlockSpec((META_ROWS, LANES), lambda b, hp, *_: (0, 2 * n_hb + hp)),
        ],
        out_specs=pl.BlockSpec((seq, LANES), lambda b, hp, *_: (b, hp)),
        scratch_shapes=[
            pltpu.VMEM((SB_T + seq, LANES), BF16),
            pltpu.VMEM((SB_T + seq, LANES), BF16),
            pltpu.VMEM((HEADS_PER_BLOCK, seq, LANES), BF16),
            pltpu.VMEM((3, SB_T, SB_T), F32),
            pltpu.VMEM((SB_T, SB_T), BF16),
            pltpu.VMEM((SB_SLOTS, HEADS_PER_BLOCK, SB_T, SB_T), F32),
            pltpu.VMEM((SB_SLOTS, HEADS_PER_BLOCK, SB_T, SB_T), BF16),
            pltpu.VMEM((SB_SLOTS, HEADS_PER_BLOCK, SB_T, SB_T), F32),
            pltpu.VMEM((SB_SLOTS, HEADS_PER_BLOCK, SB_T, SB_T), BF16),
            pltpu.VMEM((HEADS_PER_BLOCK, SB_T, LANES), F32),
            pltpu.VMEM((HEADS_PER_BLOCK, SB_T, LANES), F32),
        ],
    )
    return pl.pallas_call(
        functools.partial(_sb_kernel, n_tiles=n_tiles),
        grid_spec=grid_spec,
        out_shape=jax.ShapeDtypeStruct((batch * seq, SB_WIDTH), BF16),
        compiler_params=pltpu.CompilerParams(dimension_semantics=("arbitrary", "arbitrary"),
                                             vmem_limit_bytes=VMEM_LIMIT),
        name="sb_attn",
    )(*[jnp.asarray(a) for a in sched], sb, sb, sb, msb, msb)


def _masked_lane_max(x, mask):
    return jnp.max(jnp.where(mask, x, -jnp.inf), -1, keepdims=True)


def _first_lane_eq(x, val, mask, lane):
    return jnp.min(jnp.where(mask & (x == val), lane, LANES), -1, keepdims=True)


def _route(logits):
    lane = lax.broadcasted_iota(jnp.int32, logits.shape, 1)
    gmask = lane < N_GROUPS
    gmax = _masked_lane_max(logits, gmask)
    g_idx = _first_lane_eq(logits, gmax, gmask, lane)
    g_prob = 1.0 / jnp.sum(jnp.where(gmask, jnp.exp(logits - gmax), 0.0), -1, keepdims=True)
    lo = ROUTER_COL0 + g_idx * EXPERTS_PER_GROUP
    emask = (lane >= lo) & (lane < lo + EXPERTS_PER_GROUP)
    t1 = _masked_lane_max(logits, emask)
    i1 = _first_lane_eq(logits, t1, emask, lane)
    emask2 = emask & (lane != i1)
    t2 = _masked_lane_max(logits, emask2)
    i2 = _first_lane_eq(logits, t2, emask2, lane)
    e = jnp.exp(t2 - t1)
    w1 = g_prob / (1.0 + e)
    w2 = g_prob * e / (1.0 + e)
    return jnp.where(lane == i1, w1, 0.0) + jnp.where(lane == i2, w2, 0.0)


def _mix_out_kernel(x_ref, odn_ref, osb_ref, g0_ref, b0_ref, wzg_ref, bg_ref, wbdn_ref, wbsb_ref, wout_ref,
                    g1_ref, b1_ref, wr_ref, br_ref, h1_ref, comb_ref):
    h0 = _layer_norm(x_ref[...], g0_ref[...], b0_ref[...])
    zg = _dot(h0.astype(BF16), wzg_ref[...])
    z = zg[:, :DN_WIDTH]
    o_dn = odn_ref[...] * _silu(z)
    a = _dot(o_dn.astype(BF16), wbdn_ref[...])
    b = _dot(osb_ref[...], wbsb_ref[...])
    gate_a = _sigmoid(zg[:, DN_WIDTH:DN_WIDTH + D_MODEL] + bg_ref[0:1, :])
    gate_b = _sigmoid(zg[:, DN_WIDTH + D_MODEL:] + bg_ref[1:2, :])
    merged = gate_a * a + gate_b * b
    mix = _dot(merged.astype(BF16), wout_ref[...])
    h1 = _layer_norm(DEEPNORM_ALPHA * h0 + mix, g1_ref[...], b1_ref[...])
    h1_ref[...] = h1
    logits = _dot(h1.astype(BF16), wr_ref[...]) + br_ref[...]
    comb_ref[...] = _route(logits)


def _mix_out(x2, o_dn, o_sb, g0, b0, wzg, bg, wbdn, wbsb, wout, g1, b1, wr, br, *, tm):
    rows = x2.shape[0]
    const = lambda i: (0, 0)
    row = lambda i: (i, 0)
    full = lambda a: pl.BlockSpec(a.shape, const)
    return pl.pallas_call(
        _mix_out_kernel,
        grid=(rows // tm,),
        in_specs=[
            pl.BlockSpec((tm, D_MODEL), row),
            pl.BlockSpec((tm, DN_WIDTH), row),
            pl.BlockSpec((tm, SB_WIDTH), row),
            full(g0), full(b0), full(wzg), full(bg), full(wbdn), full(wbsb), full(wout), full(g1), full(b1),
            full(wr), full(br),
        ],
        out_specs=[pl.BlockSpec((tm, D_MODEL), row), pl.BlockSpec((tm, LANES), row)],
        out_shape=[jax.ShapeDtypeStruct((rows, D_MODEL), F32), jax.ShapeDtypeStruct((rows, LANES), F32)],
        compiler_params=pltpu.CompilerParams(dimension_semantics=("arbitrary",), vmem_limit_bytes=VMEM_LIMIT),
        name="mix_out",
    )(x2, o_dn, o_sb, g0, b0, wzg, bg, wbdn, wbsb, wout, g1, b1, wr, br)


def _moe_kernel(h1_ref, comb_ref, wg_ref, wu_ref, wd_ref, g2_ref, b2_ref, o_ref, acc_ref):
    g = pl.program_id(1)
    h1 = h1_ref[...]
    hb = h1.astype(BF16)
    hid = _silu(_dot(hb, wg_ref[0])) * _dot(hb, wu_ref[0])
    comb = comb_ref[...]
    lane = lax.broadcasted_iota(jnp.int32, comb.shape, 1)
    parts = []
    for e in range(EXPERTS_PER_GROUP):
        col = ROUTER_COL0 + g * EXPERTS_PER_GROUP + e
        c = jnp.sum(jnp.where(lane == col, comb, 0.0), -1, keepdims=True)
        parts.append(hid[:, e * EXPERT_FF:(e + 1) * EXPERT_FF] * c)
    hid = jnp.concatenate(parts, axis=-1)
    y = _dot(hid.astype(BF16), wd_ref[0])

    @pl.when(g == 0)
    def _():
        acc_ref[...] = y

    @pl.when(g > 0)
    def _():
        acc_ref[...] += y

    @pl.when(g == N_GROUPS - 1)
    def _():
        o_ref[...] = _layer_norm(DEEPNORM_ALPHA * h1 + acc_ref[...], g2_ref[...], b2_ref[...])


def _moe(h1, comb, wg, wu, wd, g2, b2, *, tm):
    rows = h1.shape[0]
    const = lambda i, g: (0, 0)
    row = lambda i, g: (i, 0)
    grp = lambda i, g: (g, 0, 0)
    return pl.pallas_call(
        _moe_kernel,
        grid=(rows // tm, N_GROUPS),
        in_specs=[
            pl.BlockSpec((tm, D_MODEL), row),
            pl.BlockSpec((tm, LANES), row),
            pl.BlockSpec((1,) + wg.shape[1:], grp),
            pl.BlockSpec((1,) + wu.shape[1:], grp),
            pl.BlockSpec((1,) + wd.shape[1:], grp),
            pl.BlockSpec((1, D_MODEL), const),
            pl.BlockSpec((1, D_MODEL), const),
        ],
        out_specs=pl.BlockSpec((tm, D_MODEL), row),
        out_shape=jax.ShapeDtypeStruct((rows, D_MODEL), F32),
        scratch_shapes=[pltpu.VMEM((tm, D_MODEL), F32)],
        compiler_params=pltpu.CompilerParams(dimension_semantics=("arbitrary", "arbitrary"),
                                             vmem_limit_bytes=VMEM_LIMIT),
        name="moe",
    )(h1, comb, wg, wu, wd, g2, b2)


def _pad_lanes(a, col0=0):
    return jnp.pad(a, ((0, 0), (col0, LANES - col0 - a.shape[1])))


def kernel(x, meta_tokens, ln_emb_g, ln_emb_b, w_in, b_gate, dn_conv_w, dn_a_log, dn_dt_bias, dn_norm_g,
           w_branch_dn, w_branch_sb, w_out, ln1_g, ln1_b, router_group_w, router_group_b, router_expert_w,
           router_expert_b, expert_w_gate, expert_w_up, expert_w_down, ln2_g, ln2_b):
    batch, seq, d = x.shape
    assert d == D_MODEL and w_in.shape[0] == 1 and seq % max(GDN_BLOCK, SB_T) == 0
    rows = batch * seq
    tm = 512
    assert rows % tm == 0
    row1 = lambda a: a.reshape(1, -1).astype(F32)

    w = w_in[0]
    c0 = 3 * DN_WIDTH
    c1 = c0 + DN_WIDTH
    c2 = c1 + 2 * DN_HEADS
    c3 = c2 + 3 * SB_WIDTH
    w_dn = w[:, :c0].astype(BF16)
    w_ba = _pad_lanes(w[:, c1:c2]).astype(BF16)
    w_sb = w[:, c2:c3].astype(BF16)
    w_zg = jnp.concatenate([w[:, c0:c1], w[:, c3:]], axis=1).astype(BF16)

    x2 = x.reshape(rows, d)
    g0, b0 = row1(ln_emb_g), row1(ln_emb_b)
    dn, sb, ba = _ln_proj(x2, g0, b0, w_dn, w_sb, w_ba, tm=tm)
    xm = jnp.concatenate([jnp.zeros((META_PAD, d), x.dtype), meta_tokens.astype(x.dtype)], axis=0)
    mdn, msb, mba = _ln_proj(xm, g0, b0, w_dn, w_sb, w_ba, tm=META_ROWS, n_zero=META_PAD)

    alog_row = _pad_lanes(dn_a_log[0].reshape(1, -1).astype(F32), DN_HEADS)
    dtb_row = _pad_lanes(dn_dt_bias[0].reshape(1, -1).astype(F32), DN_HEADS)
    o_dn = _gdn(dn, ba, mdn, mba, dn_conv_w[0].astype(F32), alog_row, dtb_row, row1(dn_norm_g[0]),
                batch=batch, seq=seq)
    o_sb = _sb_attn(sb, msb, batch=batch, seq=seq)

    w_r = _pad_lanes(jnp.concatenate(
        [router_group_w[0], router_expert_w[0].transpose(1, 0, 2).reshape(d, N_EXPERTS)], axis=1)).astype(BF16)
    b_r = _pad_lanes(jnp.concatenate(
        [router_group_b[0].reshape(1, -1), router_expert_b[0].reshape(1, -1)], axis=1).astype(F32))
    h1, comb = _mix_out(x2, o_dn, o_sb, g0, b0, w_zg, b_gate[0].astype(F32), w_branch_dn[0].astype(BF16),
                        w_branch_sb[0].astype(BF16), w_out[0].astype(BF16), row1(ln1_g[0]), row1(ln1_b[0]),
                        w_r, b_r, tm=tm)

    ef = EXPERTS_PER_GROUP * EXPERT_FF
    wg = expert_w_gate[0].transpose(0, 2, 1, 3).reshape(N_GROUPS, d, ef).astype(BF16)
    wu = expert_w_up[0].transpose(0, 2, 1, 3).reshape(N_GROUPS, d, ef).astype(BF16)
    wd = expert_w_down[0].reshape(N_GROUPS, ef, d).astype(BF16)
    out = _moe(h1, comb, wg, wu, wd, row1(ln2_g[0]), row1(ln2_b[0]), tm=tm)
    return out.reshape(batch, seq, d)
```

```python
import functools

import jax
import jax.numpy as jnp
import numpy as np
from jax import lax
from jax.experimental import pallas as pl
from jax.experimental.pallas import tpu as pltpu

F32 = jnp.float32
BF16 = jnp.bfloat16

D_MODEL = 1024
N_META = 16
DN_HEADS = 4
DN_HEAD_DIM = 128
DN_WIDTH = DN_HEADS * DN_HEAD_DIM
DN_CONV = 4
DN_CHUNK = 64
SB_HEADS = 8
SB_HEAD_DIM = 64
SB_WIDTH = SB_HEADS * SB_HEAD_DIM
N_GROUPS = 4
EXPERTS_PER_GROUP = 8
N_EXPERTS = N_GROUPS * EXPERTS_PER_GROUP
EXPERT_FF = 256
DEEPNORM_ALPHA = 2.0 ** 0.25
LN_EPS = 1e-5
RMS_EPS = 1e-6

LANES = 128
SUBLANES = 8
META_ROWS = DN_CHUNK
META_PAD = META_ROWS - N_META
ROUTER_COL0 = N_GROUPS
VMEM_LIMIT = 56 * 1024 * 1024


def _layer_norm(x, g, b):
    mu = jnp.mean(x, -1, keepdims=True)
    xc = x - mu
    var = jnp.mean(xc * xc, -1, keepdims=True)
    return xc * lax.rsqrt(var + LN_EPS) * g + b


def _sigmoid(x):
    return 1.0 / (1.0 + jnp.exp(-x))


def _softplus(x):
    return jnp.maximum(x, 0.0) + jnp.log(1.0 + jnp.exp(-jnp.abs(x)))


def _silu(x):
    return x * _sigmoid(x)


def _dot(a, b):
    return jnp.dot(a, b, preferred_element_type=F32)


def _dot_nt(a, b):
    return lax.dot_general(a, b, (((1,), (1,)), ((), ())), preferred_element_type=F32)


def _dot_tn(a, b):
    return lax.dot_general(a, b, (((0,), (0,)), ((), ())), preferred_element_type=F32)


def _ln_proj_kernel(x_ref, g_ref, b_ref, wdn_ref, wsb_ref, wba_ref, dn_ref, sb_ref, ba_ref, *, n_zero):
    h = _layer_norm(x_ref[...], g_ref[...], b_ref[...])
    if n_zero:
        rows = lax.broadcasted_iota(jnp.int32, h.shape, 0)
        h = jnp.where(rows >= n_zero, h, 0.0)
    hb = h.astype(BF16)
    dn_ref[...] = _dot(hb, wdn_ref[...])
    sb_ref[...] = _dot(hb, wsb_ref[...]).astype(BF16)
    ba_ref[...] = _dot(hb, wba_ref[...])


def _ln_proj(x2, g, b, wdn, wsb, wba, *, tm, n_zero=0):
    rows = x2.shape[0]
    const = lambda i: (0, 0)
    row = lambda i: (i, 0)
    return pl.pallas_call(
        functools.partial(_ln_proj_kernel, n_zero=n_zero),
        grid=(rows // tm,),
        in_specs=[
            pl.BlockSpec((tm, D_MODEL), row),
            pl.BlockSpec((1, D_MODEL), const),
            pl.BlockSpec((1, D_MODEL), const),
            pl.BlockSpec(wdn.shape, const),
            pl.BlockSpec(wsb.shape, const),
            pl.BlockSpec(wba.shape, const),
        ],
        out_specs=[
            pl.BlockSpec((tm, 3 * DN_WIDTH), row),
            pl.BlockSpec((tm, 3 * SB_WIDTH), row),
            pl.BlockSpec((tm, LANES), row),
        ],
        out_shape=[
            jax.ShapeDtypeStruct((rows, 3 * DN_WIDTH), F32),
            jax.ShapeDtypeStruct((rows, 3 * SB_WIDTH), BF16),
            jax.ShapeDtypeStruct((rows, LANES), F32),
        ],
        compiler_params=pltpu.CompilerParams(dimension_semantics=("arbitrary",), vmem_limit_bytes=VMEM_LIMIT),
        name="ln_proj",
    )(x2, g, b, wdn, wsb, wba)


GDN_BLOCK = 256
CONV_HIST = 8


def _gdn_rows(src_ref, ba_ref, n, n_zero, o_ref, cw_ref, alog_ref, dtb_ref, ng_ref, xbuf, s_ref, vn_ref):
    c = DN_CHUNK
    xbuf[CONV_HIST:CONV_HIST + n, :] = src_ref[...]
    acc = xbuf[CONV_HIST:CONV_HIST + n, :] * cw_ref[DN_CONV - 1:DN_CONV, :]
    for i in range(DN_CONV - 1):
        s = DN_CONV - 1 - i
        acc = acc + xbuf[CONV_HIST - s:CONV_HIST - s + n, :] * cw_ref[i:i + 1, :]
    hist = xbuf[n:n + CONV_HIST, :]
    xbuf[0:CONV_HIST, :] = hist
    qkv = _silu(acc)

    ba = ba_ref[...]
    beta_all = _sigmoid(ba)
    g_all = -jnp.exp(alog_ref[...]) * _softplus(ba + dtb_ref[...])
    if n_zero:
        rows = lax.broadcasted_iota(jnp.int32, g_all.shape, 0)
        g_all = jnp.where(rows >= n_zero, g_all, 0.0)

    ri = lax.broadcasted_iota(jnp.int32, (n, n), 0)
    ci = lax.broadcasted_iota(jnp.int32, (n, n), 1)
    same = (ri // c) == (ci // c)
    causal = same & (ri >= ci)
    strict = same & (ri > ci)
    tril = causal.astype(BF16)
    g_hi = g_all.astype(BF16)
    g_lo = (g_all - g_hi.astype(F32)).astype(BF16)
    dec = _dot(tril, g_hi) + _dot(tril, g_lo)
    dec_t = dec.T

    vn_ref[...] = jnp.zeros_like(vn_ref)
    heads = range(DN_HEADS)
    qe, ks, d_cols, qks, rhss, ps = [], [], [], [], [], []
    for h in heads:
        q = qkv[:, h * DN_HEAD_DIM:(h + 1) * DN_HEAD_DIM]
        k = qkv[:, DN_WIDTH + h * DN_HEAD_DIM:DN_WIDTH + (h + 1) * DN_HEAD_DIM]
        v = qkv[:, 2 * DN_WIDTH + h * DN_HEAD_DIM:2 * DN_WIDTH + (h + 1) * DN_HEAD_DIM]
        q = q * lax.rsqrt(jnp.sum(q * q, -1, keepdims=True) + RMS_EPS) * (DN_HEAD_DIM ** -0.5)
        k = k * lax.rsqrt(jnp.sum(k * k, -1, keepdims=True) + RMS_EPS)
        beta = beta_all[:, h:h + 1]
        d_col = dec[:, DN_HEADS + h:DN_HEADS + h + 1]
        d_row = dec_t[DN_HEADS + h:DN_HEADS + h + 1, :]
        lmask = jnp.where(causal, jnp.exp(jnp.where(causal, d_col - d_row, 0.0)), 0.0)
        kb = k * beta
        k16 = k.astype(BF16)
        a = jnp.where(strict, _dot_nt(kb.astype(BF16), k16) * lmask, 0.0)
        qks.append((_dot_nt(q.astype(BF16), k16) * lmask).astype(BF16))
        e_col = jnp.exp(d_col)
        rhss.append(jnp.concatenate([v * beta, kb * e_col], axis=1))
        qe.append(q * e_col)
        ks.append(k)
        d_cols.append(d_col)
        ps.append(-a)
    ts = list(ps)
    for _ in range(5):
        p16s = [p.astype(BF16) for p in ps]
        ps = [_dot(p16, p16) for p16 in p16s]
        ts = [t + p + _dot(t.astype(BF16), p.astype(BF16)) for t, p in zip(ts, ps)]
    uws = [rhs + _dot(t.astype(BF16), rhs.astype(BF16)) for t, rhs in zip(ts, rhss)]

    for ch in range(n // c):
        r0 = ch * c
        for h in heads:
            k, d_col, qk, uw = ks[h], d_cols[h], qks[h], uws[h]
            d_c = d_col[r0:r0 + c, :]
            d_last = d_col[r0 + c - 1:r0 + c, :]
            s = s_ref[h]
            wq = jnp.concatenate([uw[r0:r0 + c, DN_HEAD_DIM:], qe[h][r0:r0 + c, :]], axis=0)
            r = _dot(wq.astype(BF16), s.astype(BF16))
            v_new = uw[r0:r0 + c, :DN_HEAD_DIM] - r[:c, :]
            vn16 = v_new.astype(BF16)
            vn_ref[h, r0:r0 + c, :] = vn16
            o = r[c:, :] + _dot(qk[r0:r0 + c, :], vn_ref[h])
            k_dec = k[r0:r0 + c, :] * jnp.exp(d_last - d_c)
            s_ref[h] = s * jnp.exp(d_last) + _dot_tn(k_dec.astype(BF16), vn16)
            if o_ref is not None:
                o_n = o * lax.rsqrt(jnp.mean(o * o, -1, keepdims=True) + RMS_EPS) * ng_ref[...]
                o_ref[r0:r0 + c, h * DN_HEAD_DIM:(h + 1) * DN_HEAD_DIM] = o_n


def _gdn_kernel(dn_ref, ba_ref, mdn_ref, mba_ref, cw_ref, alog_ref, dtb_ref, ng_ref, o_ref, xbuf, s_ref, vn_ref,
                vnm_ref):
    @pl.when(pl.program_id(1) == 0)
    def _():
        s_ref[...] = jnp.zeros_like(s_ref)
        xbuf[0:CONV_HIST, :] = jnp.zeros((CONV_HIST, xbuf.shape[1]), F32)
        _gdn_rows(mdn_ref, mba_ref, META_ROWS, META_PAD, None, cw_ref, alog_ref, dtb_ref, ng_ref, xbuf, s_ref,
                  vnm_ref)

    _gdn_rows(dn_ref, ba_ref, GDN_BLOCK, 0, o_ref, cw_ref, alog_ref, dtb_ref, ng_ref, xbuf, s_ref, vn_ref)


def _gdn(dn, ba, mdn, mba, conv_w, alog_row, dtb_row, norm_g, *, batch, seq):
    nb = seq // GDN_BLOCK
    const = lambda b, j: (0, 0)
    row = lambda b, j: (b * nb + j, 0)
    return pl.pallas_call(
        _gdn_kernel,
        grid=(batch, nb),
        in_specs=[
            pl.BlockSpec((GDN_BLOCK, 3 * DN_WIDTH), row),
            pl.BlockSpec((GDN_BLOCK, LANES), row),
            pl.BlockSpec((META_ROWS, 3 * DN_WIDTH), const),
            pl.BlockSpec((META_ROWS, LANES), const),
            pl.BlockSpec((DN_CONV, 3 * DN_WIDTH), const),
            pl.BlockSpec((1, LANES), const),
            pl.BlockSpec((1, LANES), const),
            pl.BlockSpec((1, DN_HEAD_DIM), const),
        ],
        out_specs=pl.BlockSpec((GDN_BLOCK, DN_WIDTH), row),
        out_shape=jax.ShapeDtypeStruct((batch * seq, DN_WIDTH), F32),
        scratch_shapes=[
            pltpu.VMEM((GDN_BLOCK + CONV_HIST, 3 * DN_WIDTH), F32),
            pltpu.VMEM((DN_HEADS, DN_HEAD_DIM, DN_HEAD_DIM), F32),
            pltpu.VMEM((DN_HEADS, GDN_BLOCK, DN_HEAD_DIM), BF16),
            pltpu.VMEM((DN_HEADS, META_ROWS, DN_HEAD_DIM), BF16),
        ],
        compiler_params=pltpu.CompilerParams(dimension_semantics=("arbitrary", "arbitrary"),
                                             vmem_limit_bytes=VMEM_LIMIT),
        name="gdn",
    )(dn, ba, mdn, mba, conv_w, alog_row, dtb_row, norm_g)


SB_T = 2 * LANES
HEADS_PER_BLOCK = LANES // SB_HEAD_DIM
SB_SLOTS = 4
SB_UNROLL = 8
NEG_BIG = -1e30


def _sb_schedule(nq):
    qoff, koff, bsel, first = [], [], [], []
    for qi in range(nq):
        for kj in list(range(qi, -1, -1)) + [-1]:
            qoff.append(qi * SB_T)
            koff.append((kj + 1) * SB_T)
            bsel.append(1 if kj == qi else (2 if kj < 0 else 0))
            first.append(1 if kj == qi else 0)
    return [np.asarray(a, np.int32) for a in (qoff, koff, bsel, first)]


def _sb_kernel(qoff_ref, koff_ref, bsel_ref, first_ref, q_ref, k_ref, v_ref, mk_ref, mv_ref, o_ref,
               kbuf, vbuf, qm_s, bias_s, nu_s, z_s, sp_s, later_s, w_s, acc_s, carry_s, *, n_tiles):
    t = SB_T
    seq = k_ref.shape[0]
    for buf, m_ref, x_ref in ((kbuf, mk_ref, k_ref), (vbuf, mv_ref, v_ref)):
        buf[0:t - META_ROWS, :] = jnp.zeros((t - META_ROWS, LANES), BF16)
        buf[t - META_ROWS:t, :] = m_ref[...]
        buf[t:t + seq, :] = x_ref[...]
    ri = lax.broadcasted_iota(jnp.int32, (t, t), 0)
    ci = lax.broadcasted_iota(jnp.int32, (t, t), 1)
    bias_s[0] = jnp.zeros((t, t), F32)
    bias_s[1] = jnp.where(ci < ri, 0.0, NEG_BIG)
    bias_s[2] = jnp.where(ci >= t - N_META, 0.0, NEG_BIG)
    acc_s[...] = jnp.zeros_like(acc_s)
    carry_s[...] = jnp.zeros_like(carry_s)
    nu_s[...] = jnp.where(ri > ci, -1.0, 0.0).astype(BF16)
    lane = lax.broadcasted_iota(jnp.int32, (t, LANES), 1)
    scale = SB_HEAD_DIM ** -0.5

    q_all = q_ref[...]
    lane_q = lax.broadcasted_iota(jnp.int32, q_all.shape, 1)
    for h in range(HEADS_PER_BLOCK):
        qm_s[h] = jnp.where((lane_q // SB_HEAD_DIM) == h, q_all, jnp.zeros_like(q_all)) * scale

    def st_logits(n, s):
        qo = pl.multiple_of(qoff_ref[n], t)
        ko = pl.multiple_of(koff_ref[n], t)
        kt = kbuf[pl.ds(ko, t), :]
        bias = bias_s[bsel_ref[n]]
        for h in range(HEADS_PER_BLOCK):
            z_s[s, h] = _dot_nt(qm_s[h, pl.ds(qo, t), :], kt) + bias

    def st_softplus(n, s):
        for h in range(HEADS_PER_BLOCK):
            z = z_s[s, h]
            sp = jnp.maximum(z, 0.0) + jnp.log(1.0 + jnp.exp(-jnp.abs(z)))
            sp_s[s, h] = sp.astype(BF16)
            z_s[s, h] = z - sp

    def st_keysum(n, s):
        for h in range(HEADS_PER_BLOCK):
            later_s[s, h] = _dot(sp_s[s, h], nu_s[...])

    def st_weights(n, s):
        keep = jnp.where(first_ref[n] == 1, 0.0, 1.0)
        for h in range(HEADS_PER_BLOCK):
            later = later_s[s, h]
            carry = carry_s[h] * keep
            logw = z_s[s, h] + later + jnp.concatenate([carry] * (t // LANES), axis=1)
            w_s[s, h] = jnp.exp(logw).astype(BF16)
            total = later[:, 0:1] - sp_s[s, h, :, 0:1].astype(F32)
            carry_s[h] = carry + jnp.broadcast_to(total, (t, LANES))

    def st_values(n, s):
        qo = pl.multiple_of(qoff_ref[n], t)
        ko = pl.multiple_of(koff_ref[n], t)
        vt = vbuf[pl.ds(ko, t), :]
        keep = jnp.where(first_ref[n] == 1, 0.0, 1.0)
        accs = []
        for h in range(HEADS_PER_BLOCK):
            acc = acc_s[h] * keep + _dot(w_s[s, h], vt)
            acc_s[h] = acc
            accs.append(acc)
        out = accs[0]
        for h in range(1, HEADS_PER_BLOCK):
            out = jnp.where((lane // SB_HEAD_DIM) == h, accs[h], out)
        o_ref[pl.ds(qo, t), :] = out.astype(o_ref.dtype)

    stages = (st_logits, st_softplus, st_keysum, st_weights, st_values)
    depth = len(stages)

    def trip(it, parity):
        for d in reversed(range(depth)):
            n = it - d
            if isinstance(n, int) and not 0 <= n < n_tiles:
                continue
            stages[d](n, (parity - d) % SB_SLOTS)

    for it in range(depth - 1):
        trip(it, it % SB_SLOTS)
    first_full = depth - 1
    n_full = n_tiles - first_full
    n_loop = n_full // SB_UNROLL

    def body(u, c):
        base = first_full + u * SB_UNROLL
        for j in range(SB_UNROLL):
            trip(base + j, (first_full + j) % SB_SLOTS)
        return c

    lax.fori_loop(0, n_loop, body, 0)
    for it in range(first_full + n_loop * SB_UNROLL, n_tiles + depth - 1):
        trip(it, it % SB_SLOTS)


def _sb_attn(sb, msb, *, batch, seq):
    nq = seq // SB_T
    n_hb = SB_WIDTH // LANES
    sched = _sb_schedule(nq)
    n_tiles = int(sched[0].shape[0])
    grid_spec = pltpu.PrefetchScalarGridSpec(
        num_scalar_prefetch=len(sched),
        grid=(batch, n_hb),
        in_specs=[
            pl.BlockSpec((seq, LANES), lambda b, hp, *_: (b, hp)),
            pl.BlockSpec((seq, LANES), lambda b, hp, *_: (b, n_hb + hp)),
            pl.BlockSpec((seq, LANES), lambda b, hp, *_: (b, 2 * n_hb + hp)),
            pl.BlockSpec((META_ROWS, LANES), lambda b, hp, *_: (0, n_hb + hp)),
            pl.BlockSpec((META_ROWS, LANES), lambda b, hp, *_: (0, 2 * n_hb + hp)),
        ],
        out_specs=pl.BlockSpec((seq, LANES), lambda b, hp, *_: (b, hp)),
        scratch_shapes=[
            pltpu.VMEM((SB_T + seq, LANES), BF16),
            pltpu.VMEM((SB_T + seq, LANES), BF16),
            pltpu.VMEM((HEADS_PER_BLOCK, seq, LANES), BF16),
            pltpu.VMEM((3, SB_T, SB_T), F32),
            pltpu.VMEM((SB_T, SB_T), BF16),
            pltpu.VMEM((SB_SLOTS, HEADS_PER_BLOCK, SB_T, SB_T), F32),
            pltpu.VMEM((SB_SLOTS, HEADS_PER_BLOCK, SB_T, SB_T), BF16),
            pltpu.VMEM((SB_SLOTS, HEADS_PER_BLOCK, SB_T, SB_T), F32),
            pltpu.VMEM((SB_SLOTS, HEADS_PER_BLOCK, SB_T, SB_T), BF16),
            pltpu.VMEM((HEADS_PER_BLOCK, SB_T, LANES), F32),
            pltpu.VMEM((HEADS_PER_BLOCK, SB_T, LANES), F32),
        ],
    )
    return pl.pallas_call(
        functools.partial(_sb_kernel, n_tiles=n_tiles),
        grid_spec=grid_spec,
        out_shape=jax.ShapeDtypeStruct((batch * seq, SB_WIDTH), BF16),
        compiler_params=pltpu.CompilerParams(dimension_semantics=("arbitrary", "arbitrary"),
                                             vmem_limit_bytes=VMEM_LIMIT),
        name="sb_attn",
    )(*[jnp.asarray(a) for a in sched], sb, sb, sb, msb, msb)


def _masked_lane_max(x, mask):
    return jnp.max(jnp.where(mask, x, -jnp.inf), -1, keepdims=True)


def _first_lane_eq(x, val, mask, lane):
    return jnp.min(jnp.where(mask & (x == val), lane, LANES), -1, keepdims=True)


def _route(logits):
    lane = lax.broadcasted_iota(jnp.int32, logits.shape, 1)
    gmask = lane < N_GROUPS
    gmax = _masked_lane_max(logits, gmask)
    g_idx = _first_lane_eq(logits, gmax, gmask, lane)
    g_prob = 1.0 / jnp.sum(jnp.where(gmask, jnp.exp(logits - gmax), 0.0), -1, keepdims=True)
    lo = ROUTER_COL0 + g_idx * EXPERTS_PER_GROUP
    emask = (lane >= lo) & (lane < lo + EXPERTS_PER_GROUP)
    t1 = _masked_lane_max(logits, emask)
    i1 = _first_lane_eq(logits, t1, emask, lane)
    emask2 = emask & (lane != i1)
    t2 = _masked_lane_max(logits, emask2)
    i2 = _first_lane_eq(logits, t2, emask2, lane)
    e = jnp.exp(t2 - t1)
    w1 = g_prob / (1.0 + e)
    w2 = g_prob * e / (1.0 + e)
    comb = jnp.where(lane == i1, w1, 0.0) + jnp.where(lane == i2, w2, 0.0)
    comb = jnp.where(lane == 0, g_idx.astype(F32), comb)
    counts = jnp.sum((lane == g_idx).astype(F32), 0, keepdims=True)
    return comb, counts


def _mix_out_kernel(x_ref, odn_ref, osb_ref, g0_ref, b0_ref, wzg_ref, bg_ref, wbdn_ref, wbsb_ref, wout_ref,
                    g1_ref, b1_ref, wr_ref, br_ref, h1_ref, comb_ref, cnt_ref):
    h0 = _layer_norm(x_ref[...], g0_ref[...], b0_ref[...])
    zg = _dot(h0.astype(BF16), wzg_ref[...])
    z = zg[:, :DN_WIDTH]
    o_dn = odn_ref[...] * _silu(z)
    a = _dot(o_dn.astype(BF16), wbdn_ref[...])
    b = _dot(osb_ref[...], wbsb_ref[...])
    gate_a = _sigmoid(zg[:, DN_WIDTH:DN_WIDTH + D_MODEL] + bg_ref[0:1, :])
    gate_b = _sigmoid(zg[:, DN_WIDTH + D_MODEL:] + bg_ref[1:2, :])
    merged = gate_a * a + gate_b * b
    mix = _dot(merged.astype(BF16), wout_ref[...])
    h1 = _layer_norm(DEEPNORM_ALPHA * h0 + mix, g1_ref[...], b1_ref[...])
    h1_ref[...] = h1
    logits = _dot(h1.astype(BF16), wr_ref[...]) + br_ref[...]
    comb, counts = _route(logits)
    comb_ref[...] = comb
    cnt_ref[0] = jnp.broadcast_to(counts, cnt_ref.shape[1:])


def _mix_out(x2, o_dn, o_sb, g0, b0, wzg, bg, wbdn, wbsb, wout, g1, b1, wr, br, *, tm):
    rows = x2.shape[0]
    const = lambda i: (0, 0)
    row = lambda i: (i, 0)
    full = lambda a: pl.BlockSpec(a.shape, const)
    return pl.pallas_call(
        _mix_out_kernel,
        grid=(rows // tm,),
        in_specs=[
            pl.BlockSpec((tm, D_MODEL), row),
            pl.BlockSpec((tm, DN_WIDTH), row),
            pl.BlockSpec((tm, SB_WIDTH), row),
            full(g0), full(b0), full(wzg), full(bg), full(wbdn), full(wbsb), full(wout), full(g1), full(b1),
            full(wr), full(br),
        ],
        out_specs=[pl.BlockSpec((tm, D_MODEL), row), pl.BlockSpec((tm, LANES), row),
                   pl.BlockSpec((1, SUBLANES, LANES), lambda i: (i, 0, 0))],
        out_shape=[jax.ShapeDtypeStruct((rows, D_MODEL), F32), jax.ShapeDtypeStruct((rows, LANES), F32),
                   jax.ShapeDtypeStruct((rows // tm, SUBLANES, LANES), F32)],
        compiler_params=pltpu.CompilerParams(dimension_semantics=("arbitrary",), vmem_limit_bytes=VMEM_LIMIT),
        name="mix_out",
    )(x2, o_dn, o_sb, g0, b0, wzg, bg, wbdn, wbsb, wout, g1, b1, wr, br)


MOE_TM = 512
SEG_ALIGN = 16
SEG_SIZES = (512, 256, 128, 64, 32, 16)
SORT_ROWS = MOE_TM + 64
X_EXT = D_MODEL + 2 * LANES


def _sorted_one_hot(comb, soff_ref, i):
    tm = comb.shape[0]
    lane = lax.broadcasted_iota(jnp.int32, (tm, LANES), 1)
    onehot = lane == comb[:, 0:1].astype(jnp.int32)
    ri = lax.broadcasted_iota(jnp.int32, (tm, tm), 0)
    ci = lax.broadcasted_iota(jnp.int32, (tm, tm), 1)
    ranks = _dot((ri > ci).astype(BF16), onehot.astype(BF16))
    start = jnp.zeros((1, LANES), F32)
    for g in range(N_GROUPS):
        start = jnp.where(lane[0:1, :] == g, soff_ref[i * N_GROUPS + g].astype(F32), start)
    pos = jnp.sum(jnp.where(onehot, ranks + start, 0.0), -1, keepdims=True)
    slot = lax.broadcasted_iota(jnp.int32, (tm, SORT_ROWS), 1)
    return (slot == pos.astype(jnp.int32)).astype(BF16)


def _for_each_piece(length, fn):
    for size in SEG_SIZES:
        @pl.when((length & size) != 0)
        def _(size=size):
            fn(pl.multiple_of(length & (-2 * size), SEG_ALIGN), size)


def _segment_copies(i, len_ref, src_ref, soff_ref, dst_ref, doff_ref, sem, act):
    for g in range(N_GROUPS):
        n = i * N_GROUPS + g
        so = soff_ref[n]
        do = doff_ref[n]

        def piece(off, size, so=so, do=do):
            act(pltpu.make_async_copy(src_ref.at[pl.ds(pl.multiple_of(so + off, SEG_ALIGN), size)],
                                      dst_ref.at[pl.ds(pl.multiple_of(do + off, SEG_ALIGN), size)], sem))

        _for_each_piece(len_ref[n], piece)


def _dispatch_kernel(len_ref, soff_ref, doff_ref, tlen_ref, toff_ref, nv_ref, h1_ref, comb_ref, xg_ref, xs_ref,
                     zero_ref, sem):
    i = pl.program_id(0)
    comb = comb_ref[...]
    pt = _sorted_one_hot(comb, soff_ref, i)
    lane = lax.broadcasted_iota(jnp.int32, comb.shape, 1)
    c = jnp.where(lane >= ROUTER_COL0, comb, 0.0)
    c_hi = c.astype(BF16)
    c_lo = (c - c_hi.astype(F32)).astype(BF16)
    src = jnp.concatenate([h1_ref[...].astype(BF16), c_hi, c_lo], axis=1)
    xs_ref[...] = _dot_tn(pt, src).astype(BF16)
    to_groups = (i, len_ref, xs_ref, soff_ref, xg_ref, doff_ref, sem)
    _segment_copies(*to_groups, lambda cp: cp.start())

    @pl.when(i == pl.num_programs(0) - 1)
    def _():
        zero_ref[...] = jnp.zeros_like(zero_ref)
        for act in (lambda cp: cp.start(), lambda cp: cp.wait()):
            for g in range(N_GROUPS):
                def piece(off, size, g=g, act=act):
                    act(pltpu.make_async_copy(
                        zero_ref.at[pl.ds(off, size)],
                        xg_ref.at[pl.ds(pl.multiple_of(toff_ref[g] + off, SEG_ALIGN), size)], sem))

                _for_each_piece(tlen_ref[g], piece)

        def zero_tile(k, carry):
            cp = pltpu.make_async_copy(zero_ref, xg_ref.at[pl.ds(pl.multiple_of(k * MOE_TM, MOE_TM), MOE_TM)], sem)
            cp.start()
            cp.wait()
            return carry

        lax.fori_loop(nv_ref[0], xg_ref.shape[0] // MOE_TM, zero_tile, 0)

    _segment_copies(*to_groups, lambda cp: cp.wait())


def _dispatch(h1, comb, tabs, *, n_ffn_tiles):
    rows = h1.shape[0]
    row = lambda i, *_: (i, 0)
    grid_spec = pltpu.PrefetchScalarGridSpec(
        num_scalar_prefetch=len(tabs),
        grid=(rows // MOE_TM,),
        in_specs=[pl.BlockSpec((MOE_TM, D_MODEL), row), pl.BlockSpec((MOE_TM, LANES), row)],
        out_specs=pl.BlockSpec(memory_space=pl.ANY),
        scratch_shapes=[pltpu.VMEM((SORT_ROWS, X_EXT), BF16), pltpu.VMEM((MOE_TM, X_EXT), BF16),
                        pltpu.SemaphoreType.DMA(())],
    )
    return pl.pallas_call(
        _dispatch_kernel,
        grid_spec=grid_spec,
        out_shape=jax.ShapeDtypeStruct((n_ffn_tiles * MOE_TM, X_EXT), BF16),
        compiler_params=pltpu.CompilerParams(dimension_semantics=("arbitrary",), vmem_limit_bytes=VMEM_LIMIT),
        name="moe_dispatch",
    )(*tabs, h1, comb)


def _group_ffn_kernel(tg_ref, nv_ref, x_ref, wg_ref, wu_ref, wd_ref, y_ref):
    k = pl.program_id(0)

    @pl.when(k < nv_ref[0])
    def _():
        g = tg_ref[k]
        xe = x_ref[...]
        hb = xe[:, :D_MODEL]
        comb = xe[:, D_MODEL:D_MODEL + LANES].astype(F32) + xe[:, D_MODEL + LANES:].astype(F32)
        hid = _silu(_dot(hb, wg_ref[0])) * _dot(hb, wu_ref[0])
        lane = lax.broadcasted_iota(jnp.int32, comb.shape, 1)
        parts = []
        for e in range(EXPERTS_PER_GROUP):
            col = ROUTER_COL0 + g * EXPERTS_PER_GROUP + e
            c = jnp.sum(jnp.where(lane == col, comb, 0.0), -1, keepdims=True)
            parts.append(hid[:, e * EXPERT_FF:(e + 1) * EXPERT_FF] * c)
        y_ref[...] = _dot(jnp.concatenate(parts, axis=-1).astype(BF16), wd_ref[0])

    @pl.when(k >= nv_ref[0])
    def _():
        y_ref[...] = jnp.zeros_like(y_ref)


def _group_ffn(xg, tile_group, n_valid, wg, wu, wd):
    n_tiles = xg.shape[0] // MOE_TM
    row = lambda k, tg, nv: (jnp.minimum(k, nv[0] - 1), 0)
    out_row = lambda k, tg, nv: (k, 0)
    grp = lambda k, tg, nv: (tg[k], 0, 0)
    grid_spec = pltpu.PrefetchScalarGridSpec(
        num_scalar_prefetch=2,
        grid=(n_tiles,),
        in_specs=[
            pl.BlockSpec((MOE_TM, X_EXT), row),
            pl.BlockSpec((1,) + wg.shape[1:], grp),
            pl.BlockSpec((1,) + wu.shape[1:], grp),
            pl.BlockSpec((1,) + wd.shape[1:], grp),
        ],
        out_specs=pl.BlockSpec((MOE_TM, D_MODEL), out_row),
    )
    return pl.pallas_call(
        _group_ffn_kernel,
        grid_spec=grid_spec,
        out_shape=jax.ShapeDtypeStruct((n_tiles * MOE_TM, D_MODEL), F32),
        compiler_params=pltpu.CompilerParams(dimension_semantics=("arbitrary",), vmem_limit_bytes=VMEM_LIMIT),
        name="moe_ffn",
    )(tile_group, n_valid, xg, wg, wu, wd)


def _combine_kernel(len_ref, soff_ref, doff_ref, h1_ref, comb_ref, g2_ref, b2_ref, ys_ref, o_ref, ysb_ref, sem):
    i = pl.program_id(0)
    ysb_ref[...] = jnp.zeros_like(ysb_ref)
    from_groups = (i, len_ref, ys_ref, doff_ref, ysb_ref, soff_ref, sem)
    _segment_copies(*from_groups, lambda cp: cp.start())
    pt = _sorted_one_hot(comb_ref[...], soff_ref, i)
    _segment_copies(*from_groups, lambda cp: cp.wait())
    ys = ysb_ref[...]
    hi = ys.astype(BF16)
    lo = (ys - hi.astype(F32)).astype(BF16)
    ffn = _dot(pt, hi) + _dot(pt, lo)
    o_ref[...] = _layer_norm(DEEPNORM_ALPHA * h1_ref[...] + ffn, g2_ref[...], b2_ref[...])


def _combine(h1, comb, ys, g2, b2, tabs):
    rows = h1.shape[0]
    row = lambda i, *_: (i, 0)
    const = lambda i, *_: (0, 0)
    grid_spec = pltpu.PrefetchScalarGridSpec(
        num_scalar_prefetch=len(tabs),
        grid=(rows // MOE_TM,),
        in_specs=[pl.BlockSpec((MOE_TM, D_MODEL), row), pl.BlockSpec((MOE_TM, LANES), row),
                  pl.BlockSpec((1, D_MODEL), const), pl.BlockSpec((1, D_MODEL), const),
                  pl.BlockSpec(memory_space=pl.ANY)],
        out_specs=pl.BlockSpec((MOE_TM, D_MODEL), row),
        scratch_shapes=[pltpu.VMEM((SORT_ROWS, D_MODEL), F32), pltpu.SemaphoreType.DMA(())],
    )
    return pl.pallas_call(
        _combine_kernel,
        grid_spec=grid_spec,
        out_shape=jax.ShapeDtypeStruct((rows, D_MODEL), F32),
        compiler_params=pltpu.CompilerParams(dimension_semantics=("arbitrary",), vmem_limit_bytes=VMEM_LIMIT),
        name="moe_combine",
    )(*tabs, h1, comb, g2, b2, ys)


def _moe_tables(cnt, rows):
    n_tiles = cnt.shape[0]
    up = lambda a, m: (a + m - 1) // m * m
    seg = up(cnt, SEG_ALIGN)
    soff = jnp.cumsum(seg, axis=1) - seg
    gtot = jnp.sum(seg, axis=0)
    gpad = up(gtot, MOE_TM)
    gbase = jnp.cumsum(gpad) - gpad
    doff = gbase[None, :] + jnp.cumsum(seg, axis=0) - seg
    n_ffn_tiles = -(-(rows + n_tiles * N_GROUPS * (SEG_ALIGN - 1)) // MOE_TM) + N_GROUPS
    starts = jnp.arange(n_ffn_tiles, dtype=jnp.int32) * MOE_TM
    tile_group = jnp.minimum(jnp.sum(starts[:, None] >= (gbase + gpad)[None, :], axis=1), N_GROUPS - 1)
    n_valid = (jnp.sum(gpad) // MOE_TM).reshape(1)
    i32 = lambda a: a.reshape(-1).astype(jnp.int32)
    return dict(seg=i32(seg), soff=i32(soff), doff=i32(doff), tlen=i32(gpad - gtot), toff=i32(gbase + gtot),
                tile_group=i32(tile_group), n_valid=i32(n_valid)), n_ffn_tiles


def _pad_lanes(a, col0=0):
    return jnp.pad(a, ((0, 0), (col0, LANES - col0 - a.shape[1])))


def kernel(x, meta_tokens, ln_emb_g, ln_emb_b, w_in, b_gate, dn_conv_w, dn_a_log, dn_dt_bias, dn_norm_g,
           w_branch_dn, w_branch_sb, w_out, ln1_g, ln1_b, router_group_w, router_group_b, router_expert_w,
           router_expert_b, expert_w_gate, expert_w_up, expert_w_down, ln2_g, ln2_b):
    batch, seq, d = x.shape
    assert d == D_MODEL and w_in.shape[0] == 1 and seq % max(GDN_BLOCK, SB_T) == 0
    rows = batch * seq
    tm = 512
    assert rows % tm == 0
    row1 = lambda a: a.reshape(1, -1).astype(F32)

    w = w_in[0]
    c0 = 3 * DN_WIDTH
    c1 = c0 + DN_WIDTH
    c2 = c1 + 2 * DN_HEADS
    c3 = c2 + 3 * SB_WIDTH
    w_dn = w[:, :c0].astype(BF16)
    w_ba = _pad_lanes(w[:, c1:c2]).astype(BF16)
    w_sb = w[:, c2:c3].astype(BF16)
    w_zg = jnp.concatenate([w[:, c0:c1], w[:, c3:]], axis=1).astype(BF16)

    x2 = x.reshape(rows, d)
    g0, b0 = row1(ln_emb_g), row1(ln_emb_b)
    dn, sb, ba = _ln_proj(x2, g0, b0, w_dn, w_sb, w_ba, tm=tm)
    xm = jnp.concatenate([jnp.zeros((META_PAD, d), x.dtype), meta_tokens.astype(x.dtype)], axis=0)
    mdn, msb, mba = _ln_proj(xm, g0, b0, w_dn, w_sb, w_ba, tm=META_ROWS, n_zero=META_PAD)

    alog_row = _pad_lanes(dn_a_log[0].reshape(1, -1).astype(F32), DN_HEADS)
    dtb_row = _pad_lanes(dn_dt_bias[0].reshape(1, -1).astype(F32), DN_HEADS)
    o_dn = _gdn(dn, ba, mdn, mba, dn_conv_w[0].astype(F32), alog_row, dtb_row, row1(dn_norm_g[0]),
                batch=batch, seq=seq)
    o_sb = _sb_attn(sb, msb, batch=batch, seq=seq)

    w_r = _pad_lanes(jnp.concatenate(
        [router_group_w[0], router_expert_w[0].transpose(1, 0, 2).reshape(d, N_EXPERTS)], axis=1)).astype(BF16)
    b_r = _pad_lanes(jnp.concatenate(
        [router_group_b[0].reshape(1, -1), router_expert_b[0].reshape(1, -1)], axis=1).astype(F32))
    h1, comb, cnt = _mix_out(x2, o_dn, o_sb, g0, b0, w_zg, b_gate[0].astype(F32), w_branch_dn[0].astype(BF16),
                             w_branch_sb[0].astype(BF16), w_out[0].astype(BF16), row1(ln1_g[0]),
                             row1(ln1_b[0]), w_r, b_r, tm=MOE_TM)

    ef = EXPERTS_PER_GROUP * EXPERT_FF
    wg = expert_w_gate[0].transpose(0, 2, 1, 3).reshape(N_GROUPS, d, ef).astype(BF16)
    wu = expert_w_up[0].transpose(0, 2, 1, 3).reshape(N_GROUPS, d, ef).astype(BF16)
    wd = expert_w_down[0].reshape(N_GROUPS, ef, d).astype(BF16)
    tabs, n_ffn_tiles = _moe_tables(cnt[:, 0, :N_GROUPS].astype(jnp.int32), rows)
    seg_tabs = (tabs["seg"], tabs["soff"], tabs["doff"])
    xg = _dispatch(h1, comb, seg_tabs + (tabs["tlen"], tabs["toff"], tabs["n_valid"]), n_ffn_tiles=n_ffn_tiles)
    ys = _group_ffn(xg, tabs["tile_group"], tabs["n_valid"], wg, wu, wd)
    out = _combine(h1, comb, ys, row1(ln2_g[0]), row1(ln2_b[0]), seg_tabs)
    return out.reshape(batch, seq, d)
```

```python
import functools

import jax
import jax.numpy as jnp
import numpy as np
from jax import lax
from jax.experimental import pallas as pl
from jax.experimental.pallas import tpu as pltpu

F32 = jnp.float32
BF16 = jnp.bfloat16

D_MODEL = 1024
N_META = 16
DN_HEADS = 4
DN_HEAD_DIM = 128
DN_WIDTH = DN_HEADS * DN_HEAD_DIM
DN_CONV = 4
DN_CHUNK = 64
SB_HEADS = 8
SB_HEAD_DIM = 64
SB_WIDTH = SB_HEADS * SB_HEAD_DIM
N_GROUPS = 4
EXPERTS_PER_GROUP = 8
N_EXPERTS = N_GROUPS * EXPERTS_PER_GROUP
EXPERT_FF = 256
DEEPNORM_ALPHA = 2.0 ** 0.25
LN_EPS = 1e-5
RMS_EPS = 1e-6

LANES = 128
SUBLANES = 8
META_ROWS = DN_CHUNK
META_PAD = META_ROWS - N_META
ROUTER_COL0 = N_GROUPS
VMEM_LIMIT = 56 * 1024 * 1024


def _layer_norm(x, g, b):
    mu = jnp.mean(x, -1, keepdims=True)
    xc = x - mu
    var = jnp.mean(xc * xc, -1, keepdims=True)
    return xc * lax.rsqrt(var + LN_EPS) * g + b


def _sigmoid(x):
    return 1.0 / (1.0 + jnp.exp(-x))


def _softplus(x):
    return jnp.maximum(x, 0.0) + jnp.log(1.0 + jnp.exp(-jnp.abs(x)))


def _silu(x):
    return x * _sigmoid(x)


def _dot(a, b):
    return jnp.dot(a, b, preferred_element_type=F32)


def _dot_nt(a, b):
    return lax.dot_general(a, b, (((1,), (1,)), ((), ())), preferred_element_type=F32)


def _dot_tn(a, b):
    return lax.dot_general(a, b, (((0,), (0,)), ((), ())), preferred_element_type=F32)


def _ln_proj_kernel(x_ref, g_ref, b_ref, wdn_ref, wsb_ref, wba_ref, dn_ref, sb_ref, ba_ref, *, n_zero):
    h = _layer_norm(x_ref[...], g_ref[...], b_ref[...])
    if n_zero:
        rows = lax.broadcasted_iota(jnp.int32, h.shape, 0)
        h = jnp.where(rows >= n_zero, h, 0.0)
    hb = h.astype(BF16)
    dn_ref[...] = _dot(hb, wdn_ref[...])
    sb_ref[...] = _dot(hb, wsb_ref[...]).astype(BF16)
    ba_ref[...] = _dot(hb, wba_ref[...])


def _ln_proj(x2, g, b, wdn, wsb, wba, *, tm, n_zero=0):
    rows = x2.shape[0]
    const = lambda i: (0, 0)
    row = lambda i: (i, 0)
    return pl.pallas_call(
        functools.partial(_ln_proj_kernel, n_zero=n_zero),
        grid=(rows // tm,),
        in_specs=[
            pl.BlockSpec((tm, D_MODEL), row),
            pl.BlockSpec((1, D_MODEL), const),
            pl.BlockSpec((1, D_MODEL), const),
            pl.BlockSpec(wdn.shape, const),
            pl.BlockSpec(wsb.shape, const),
            pl.BlockSpec(wba.shape, const),
        ],
        out_specs=[
            pl.BlockSpec((tm, 3 * DN_WIDTH), row),
            pl.BlockSpec((tm, 3 * SB_WIDTH), row),
            pl.BlockSpec((tm, LANES), row),
        ],
        out_shape=[
            jax.ShapeDtypeStruct((rows, 3 * DN_WIDTH), F32),
            jax.ShapeDtypeStruct((rows, 3 * SB_WIDTH), BF16),
            jax.ShapeDtypeStruct((rows, LANES), F32),
        ],
        compiler_params=pltpu.CompilerParams(dimension_semantics=("arbitrary",), vmem_limit_bytes=VMEM_LIMIT),
        name="ln_proj",
    )(x2, g, b, wdn, wsb, wba)


GDN_BLOCK = 256
GDN_STREAMS = 1
GDN_CHUNK = DN_CHUNK
CONV_HIST = 8


def _gdn_rows(streams, n, n_zero, cw_ref, alog_ref, dtb_ref, ng_ref):
    c = min(n, GDN_CHUNK)
    ri = lax.broadcasted_iota(jnp.int32, (n, n), 0)
    ci = lax.broadcasted_iota(jnp.int32, (n, n), 1)
    same = (ri // c) == (ci // c)
    causal = same & (ri >= ci)
    strict = same & (ri > ci)
    tril = causal.astype(BF16)

    pre = []
    for src_ref, ba_ref, _, xbuf, _, vn_ref in streams:
        xbuf[CONV_HIST:CONV_HIST + n, :] = src_ref[...]
        acc = xbuf[CONV_HIST:CONV_HIST + n, :] * cw_ref[DN_CONV - 1:DN_CONV, :]
        for i in range(DN_CONV - 1):
            s = DN_CONV - 1 - i
            acc = acc + xbuf[CONV_HIST - s:CONV_HIST - s + n, :] * cw_ref[i:i + 1, :]
        hist = xbuf[n:n + CONV_HIST, :]
        xbuf[0:CONV_HIST, :] = hist
        qkv = _silu(acc)

        ba = ba_ref[...]
        beta_all = _sigmoid(ba)
        g_all = -jnp.exp(alog_ref[...]) * _softplus(ba + dtb_ref[...])
        if n_zero:
            rows = lax.broadcasted_iota(jnp.int32, g_all.shape, 0)
            g_all = jnp.where(rows >= n_zero, g_all, 0.0)
        g_hi = g_all.astype(BF16)
        g_lo = (g_all - g_hi.astype(F32)).astype(BF16)
        dec = _dot(tril, g_hi) + _dot(tril, g_lo)
        vn_ref[...] = jnp.zeros_like(vn_ref)
        pre.append((qkv, beta_all, dec, dec.T))

    units = [(si, h) for si in range(len(streams)) for h in range(DN_HEADS)]
    qe, ks, d_cols, qks, rhss, ps = [], [], [], [], [], []
    for si, h in units:
        qkv, beta_all, dec, dec_t = pre[si]
        q = qkv[:, h * DN_HEAD_DIM:(h + 1) * DN_HEAD_DIM]
        k = qkv[:, DN_WIDTH + h * DN_HEAD_DIM:DN_WIDTH + (h + 1) * DN_HEAD_DIM]
        v = qkv[:, 2 * DN_WIDTH + h * DN_HEAD_DIM:2 * DN_WIDTH + (h + 1) * DN_HEAD_DIM]
        q = q * lax.rsqrt(jnp.sum(q * q, -1, keepdims=True) + RMS_EPS) * (DN_HEAD_DIM ** -0.5)
        k = k * lax.rsqrt(jnp.sum(k * k, -1, keepdims=True) + RMS_EPS)
        beta = beta_all[:, h:h + 1]
        d_col = dec[:, DN_HEADS + h:DN_HEADS + h + 1]
        d_row = dec_t[DN_HEADS + h:DN_HEADS + h + 1, :]
        lmask = jnp.where(causal, jnp.exp(jnp.where(causal, d_col - d_row, 0.0)), 0.0)
        kb = k * beta
        k16 = k.astype(BF16)
        a = jnp.where(strict, _dot_nt(kb.astype(BF16), k16) * lmask, 0.0)
        qks.append((_dot_nt(q.astype(BF16), k16) * lmask).astype(BF16))
        e_col = jnp.exp(d_col)
        rhss.append(jnp.concatenate([v * beta, kb * e_col], axis=1))
        qe.append(q * e_col)
        ks.append(k)
        d_cols.append(d_col)
        ps.append(-a)
    ts = list(ps)
    for _ in range(c.bit_length() - 2):
        p16s = [p.astype(BF16) for p in ps]
        ps = [_dot(p16, p16) for p16 in p16s]
        ts = [t + p + _dot(t.astype(BF16), p.astype(BF16)) for t, p in zip(ts, ps)]
    uws = [rhs + _dot(t.astype(BF16), rhs.astype(BF16)) for t, rhs in zip(ts, rhss)]

    for ch in range(n // c):
        r0 = ch * c
        for u, (si, h) in enumerate(units):
            _, _, o_ref, _, s_ref, vn_ref = streams[si]
            k, d_col, qk, uw = ks[u], d_cols[u], qks[u], uws[u]
            d_c = d_col[r0:r0 + c, :]
            d_last = d_col[r0 + c - 1:r0 + c, :]
            s = s_ref[h]
            wq = jnp.concatenate([uw[r0:r0 + c, DN_HEAD_DIM:], qe[u][r0:r0 + c, :]], axis=0)
            r = _dot(wq.astype(BF16), s.astype(BF16))
            v_new = uw[r0:r0 + c, :DN_HEAD_DIM] - r[:c, :]
            vn16 = v_new.astype(BF16)
            vn_ref[h, r0:r0 + c, :] = vn16
            o = r[c:, :] + _dot(qk[r0:r0 + c, :], vn_ref[h])
            k_dec = k[r0:r0 + c, :] * jnp.exp(d_last - d_c)
            s_ref[h] = s * jnp.exp(d_last) + _dot_tn(k_dec.astype(BF16), vn16)
            if o_ref is not None:
                o_n = o * lax.rsqrt(jnp.mean(o * o, -1, keepdims=True) + RMS_EPS) * ng_ref[...]
                o_ref[r0:r0 + c, h * DN_HEAD_DIM:(h + 1) * DN_HEAD_DIM] = o_n


def _gdn_kernel(dn_ref, ba_ref, mdn_ref, mba_ref, cw_ref, alog_ref, dtb_ref, ng_ref, o_ref, xbuf, s_ref, vn_ref,
                vnm_ref):
    @pl.when(pl.program_id(1) == 0)
    def _():
        s_ref[0] = jnp.zeros(s_ref.shape[1:], F32)
        xbuf[0, 0:CONV_HIST, :] = jnp.zeros((CONV_HIST, xbuf.shape[2]), F32)
        _gdn_rows([(mdn_ref, mba_ref, None, xbuf.at[0], s_ref.at[0], vnm_ref)], META_ROWS, META_PAD,
                  cw_ref, alog_ref, dtb_ref, ng_ref)
        for si in range(1, GDN_STREAMS):
            s_ref[si] = s_ref[0]
            xbuf[si, 0:CONV_HIST, :] = xbuf[0, 0:CONV_HIST, :]

    _gdn_rows([(dn_ref.at[si], ba_ref.at[si], o_ref.at[si], xbuf.at[si], s_ref.at[si], vn_ref.at[si])
               for si in range(GDN_STREAMS)], GDN_BLOCK, 0, cw_ref, alog_ref, dtb_ref, ng_ref)


def _gdn(dn, ba, mdn, mba, conv_w, alog_row, dtb_row, norm_g, *, batch, seq):
    nb = seq // GDN_BLOCK
    const = lambda b, j: (0, 0)
    row = lambda b, j: (b, j, 0)
    dn = dn.reshape(batch, seq, dn.shape[-1])
    ba = ba.reshape(batch, seq, ba.shape[-1])
    out = pl.pallas_call(
        _gdn_kernel,
        grid=(batch // GDN_STREAMS, nb),
        in_specs=[
            pl.BlockSpec((GDN_STREAMS, GDN_BLOCK, 3 * DN_WIDTH), row),
            pl.BlockSpec((GDN_STREAMS, GDN_BLOCK, LANES), row),
            pl.BlockSpec((META_ROWS, 3 * DN_WIDTH), const),
            pl.BlockSpec((META_ROWS, LANES), const),
            pl.BlockSpec((DN_CONV, 3 * DN_WIDTH), const),
            pl.BlockSpec((1, LANES), const),
            pl.BlockSpec((1, LANES), const),
            pl.BlockSpec((1, DN_HEAD_DIM), const),
        ],
        out_specs=pl.BlockSpec((GDN_STREAMS, GDN_BLOCK, DN_WIDTH), row),
        out_shape=jax.ShapeDtypeStruct((batch, seq, DN_WIDTH), F32),
        scratch_shapes=[
            pltpu.VMEM((GDN_STREAMS, GDN_BLOCK + CONV_HIST, 3 * DN_WIDTH), F32),
            pltpu.VMEM((GDN_STREAMS, DN_HEADS, DN_HEAD_DIM, DN_HEAD_DIM), F32),
            pltpu.VMEM((GDN_STREAMS, DN_HEADS, GDN_BLOCK, DN_HEAD_DIM), BF16),
            pltpu.VMEM((DN_HEADS, META_ROWS, DN_HEAD_DIM), BF16),
        ],
        compiler_params=pltpu.CompilerParams(dimension_semantics=("arbitrary", "arbitrary"),
                                             vmem_limit_bytes=VMEM_LIMIT),
        name="gdn",
    )(dn, ba, mdn, mba, conv_w, alog_row, dtb_row, norm_g)
    return out.reshape(batch * seq, DN_WIDTH)


SB_T = 2 * LANES
HEADS_PER_BLOCK = LANES // SB_HEAD_DIM
SB_SLOTS = 4
SB_UNROLL = 8
NEG_BIG = -1e30


def _sb_schedule(nq):
    qoff, koff, bsel, first = [], [], [], []
    for qi in range(nq):
        for kj in list(range(qi, -1, -1)) + [-1]:
            qoff.append(qi * SB_T)
            koff.append((kj + 1) * SB_T)
            bsel.append(1 if kj == qi else (2 if kj < 0 else 0))
            first.append(1 if kj == qi else 0)
    return [np.asarray(a, np.int32) for a in (qoff, koff, bsel, first)]


def _sb_kernel(qoff_ref, koff_ref, bsel_ref, first_ref, q_ref, k_ref, v_ref, mk_ref, mv_ref, o_ref,
               kbuf, vbuf, qm_s, bias_s, nu_s, z_s, sp_s, later_s, w_s, acc_s, carry_s, *, n_tiles):
    t = SB_T
    seq = k_ref.shape[0]
    for buf, m_ref, x_ref in ((kbuf, mk_ref, k_ref), (vbuf, mv_ref, v_ref)):
        buf[0:t - META_ROWS, :] = jnp.zeros((t - META_ROWS, LANES), BF16)
        buf[t - META_ROWS:t, :] = m_ref[...]
        buf[t:t + seq, :] = x_ref[...]
    ri = lax.broadcasted_iota(jnp.int32, (t, t), 0)
    ci = lax.broadcasted_iota(jnp.int32, (t, t), 1)
    bias_s[0] = jnp.zeros((t, t), F32)
    bias_s[1] = jnp.where(ci < ri, 0.0, NEG_BIG)
    bias_s[2] = jnp.where(ci >= t - N_META, 0.0, NEG_BIG)
    acc_s[...] = jnp.zeros_like(acc_s)
    carry_s[...] = jnp.zeros_like(carry_s)
    nu_s[...] = jnp.where(ri > ci, -1.0, 0.0).astype(BF16)
    lane = lax.broadcasted_iota(jnp.int32, (t, LANES), 1)
    scale = SB_HEAD_DIM ** -0.5

    q_all = q_ref[...]
    lane_q = lax.broadcasted_iota(jnp.int32, q_all.shape, 1)
    for h in range(HEADS_PER_BLOCK):
        qm_s[h] = jnp.where((lane_q // SB_HEAD_DIM) == h, q_all, jnp.zeros_like(q_all)) * scale

    def st_logits(n, s):
        qo = pl.multiple_of(qoff_ref[n], t)
        ko = pl.multiple_of(koff_ref[n], t)
        kt = kbuf[pl.ds(ko, t), :]
        bias = bias_s[bsel_ref[n]]
        for h in range(HEADS_PER_BLOCK):
            z_s[s, h] = _dot_nt(qm_s[h, pl.ds(qo, t), :], kt) + bias

    def st_softplus(n, s):
        for h in range(HEADS_PER_BLOCK):
            z = z_s[s, h]
            sp = jnp.maximum(z, 0.0) + jnp.log(1.0 + jnp.exp(-jnp.abs(z)))
            sp_s[s, h] = sp.astype(BF16)
            z_s[s, h] = z - sp

    def st_keysum(n, s):
        for h in range(HEADS_PER_BLOCK):
            later_s[s, h] = _dot(sp_s[s, h], nu_s[...])

    def st_weights(n, s):
        keep = jnp.where(first_ref[n] == 1, 0.0, 1.0)
        for h in range(HEADS_PER_BLOCK):
            later = later_s[s, h]
            carry = carry_s[h] * keep
            logw = z_s[s, h] + later + jnp.concatenate([carry] * (t // LANES), axis=1)
            w_s[s, h] = jnp.exp(logw).astype(BF16)
            total = later[:, 0:1] - sp_s[s, h, :, 0:1].astype(F32)
            carry_s[h] = carry + jnp.broadcast_to(total, (t, LANES))

    def st_values(n, s):
        qo = pl.multiple_of(qoff_ref[n], t)
        ko = pl.multiple_of(koff_ref[n], t)
        vt = vbuf[pl.ds(ko, t), :]
        keep = jnp.where(first_ref[n] == 1, 0.0, 1.0)
        accs = []
        for h in range(HEADS_PER_BLOCK):
            acc = acc_s[h] * keep + _dot(w_s[s, h], vt)
            acc_s[h] = acc
            accs.append(acc)
        out = accs[0]
        for h in range(1, HEADS_PER_BLOCK):
            out = jnp.where((lane // SB_HEAD_DIM) == h, accs[h], out)
        o_ref[pl.ds(qo, t), :] = out.astype(o_ref.dtype)

    stages = (st_logits, st_softplus, st_keysum, st_weights, st_values)
    depth = len(stages)

    def trip(it, parity):
        for d in reversed(range(depth)):
            n = it - d
            if isinstance(n, int) and not 0 <= n < n_tiles:
                continue
            stages[d](n, (parity - d) % SB_SLOTS)

    for it in range(depth - 1):
        trip(it, it % SB_SLOTS)
    first_full = depth - 1
    n_full = n_tiles - first_full
    n_loop = n_full // SB_UNROLL

    def body(u, c):
        base = first_full + u * SB_UNROLL
        for j in range(SB_UNROLL):
            trip(base + j, (first_full + j) % SB_SLOTS)
        return c

    lax.fori_loop(0, n_loop, body, 0)
    for it in range(first_full + n_loop * SB_UNROLL, n_tiles + depth - 1):
        trip(it, it % SB_SLOTS)


def _sb_attn(sb, msb, *, batch, seq):
    nq = seq // SB_T
    n_hb = SB_WIDTH // LANES
    sched = _sb_schedule(nq)
    n_tiles = int(sched[0].shape[0])
    grid_spec = pltpu.PrefetchScalarGridSpec(
        num_scalar_prefetch=len(sched),
        grid=(batch, n_hb),
        in_specs=[
            pl.BlockSpec((seq, LANES), lambda b, hp, *_: (b, hp)),
            pl.BlockSpec((seq, LANES), lambda b, hp, *_: (b, n_hb + hp)),
            pl.BlockSpec((seq, LANES), lambda b, hp, *_: (b, 2 * n_hb + hp)),
            pl.BlockSpec((META_ROWS, LANES), lambda b, hp, *_: (0, n_hb + hp)),
            pl.BlockSpec((META_ROWS, LANES), lambda b, hp, *_: (0, 2 * n_hb + hp)),
        ],
        out_specs=pl.BlockSpec((seq, LANES), lambda b, hp, *_: (b, hp)),
        scratch_shapes=[
            pltpu.VMEM((SB_T + seq, LANES), BF16),
            pltpu.VMEM((SB_T + seq, LANES), BF16),
            pltpu.VMEM((HEADS_PER_BLOCK, seq, LANES), BF16),
            pltpu.VMEM((3, SB_T, SB_T), F32),
            pltpu.VMEM((SB_T, SB_T), BF16),
            pltpu.VMEM((SB_SLOTS, HEADS_PER_BLOCK, SB_T, SB_T), F32),
            pltpu.VMEM((SB_SLOTS, HEADS_PER_BLOCK, SB_T, SB_T), BF16),
            pltpu.VMEM((SB_SLOTS, HEADS_PER_BLOCK, SB_T, SB_T), F32),
            pltpu.VMEM((SB_SLOTS, HEADS_PER_BLOCK, SB_T, SB_T), BF16),
            pltpu.VMEM((HEADS_PER_BLOCK, SB_T, LANES), F32),
            pltpu.VMEM((HEADS_PER_BLOCK, SB_T, LANES), F32),
        ],
    )
    return pl.pallas_call(
        functools.partial(_sb_kernel, n_tiles=n_tiles),
        grid_spec=grid_spec,
        out_shape=jax.ShapeDtypeStruct((batch * seq, SB_WIDTH), BF16),
        compiler_params=pltpu.CompilerParams(dimension_semantics=("arbitrary", "arbitrary"),
                                             vmem_limit_bytes=VMEM_LIMIT),
        name="sb_attn",
    )(*[jnp.asarray(a) for a in sched], sb, sb, sb, msb, msb)


MIX_CHAINS = 2


def _masked_lane_max(x, mask):
    return jnp.max(jnp.where(mask, x, -jnp.inf), -1, keepdims=True)


def _first_lane_eq(x, val, mask, lane):
    return jnp.min(jnp.where(mask & (x == val), lane, LANES), -1, keepdims=True)


def _route(logits):
    lane = lax.broadcasted_iota(jnp.int32, logits.shape, 1)
    gmask = lane < N_GROUPS
    gmax = _masked_lane_max(logits, gmask)
    g_idx = _first_lane_eq(logits, gmax, gmask, lane)
    g_prob = 1.0 / jnp.sum(jnp.where(gmask, jnp.exp(logits - gmax), 0.0), -1, keepdims=True)
    lo = ROUTER_COL0 + g_idx * EXPERTS_PER_GROUP
    emask = (lane >= lo) & (lane < lo + EXPERTS_PER_GROUP)
    t1 = _masked_lane_max(logits, emask)
    i1 = _first_lane_eq(logits, t1, emask, lane)
    emask2 = emask & (lane != i1)
    t2 = _masked_lane_max(logits, emask2)
    i2 = _first_lane_eq(logits, t2, emask2, lane)
    e = jnp.exp(t2 - t1)
    w1 = g_prob / (1.0 + e)
    w2 = g_prob * e / (1.0 + e)
    comb = jnp.where(lane == i1, w1, 0.0) + jnp.where(lane == i2, w2, 0.0)
    comb = jnp.where(lane == 0, g_idx.astype(F32), comb)
    counts = jnp.sum((lane == g_idx).astype(F32), 0, keepdims=True)
    return comb, counts


def _mix_out_kernel(x_ref, odn_ref, osb_ref, g0_ref, b0_ref, wzg_ref, bg_ref, wbdn_ref, wbsb_ref, wout_ref,
                    g1_ref, b1_ref, wr_ref, br_ref, h1_ref, comb_ref, cnt_ref):
    sub = x_ref.shape[0] // MIX_CHAINS
    rows = [slice(r0, r0 + sub) for r0 in range(0, x_ref.shape[0], sub)]
    h0s = [_layer_norm(x_ref[rs, :], g0_ref[...], b0_ref[...]) for rs in rows]
    zgs = [_dot(h0.astype(BF16), wzg_ref[...]) for h0 in h0s]
    bs = [_dot(osb_ref[rs, :], wbsb_ref[...]) for rs in rows]
    o_dns = [odn_ref[rs, :] * _silu(zg[:, :DN_WIDTH]) for rs, zg in zip(rows, zgs)]
    as_ = [_dot(o_dn.astype(BF16), wbdn_ref[...]) for o_dn in o_dns]
    merged = [_sigmoid(zg[:, DN_WIDTH:DN_WIDTH + D_MODEL] + bg_ref[0:1, :]) * a
              + _sigmoid(zg[:, DN_WIDTH + D_MODEL:] + bg_ref[1:2, :]) * b for zg, a, b in zip(zgs, as_, bs)]
    mixes = [_dot(m.astype(BF16), wout_ref[...]) for m in merged]
    h1s = [_layer_norm(DEEPNORM_ALPHA * h0 + mix, g1_ref[...], b1_ref[...]) for h0, mix in zip(h0s, mixes)]
    logits = [_dot(h1.astype(BF16), wr_ref[...]) + br_ref[...] for h1 in h1s]
    counts = jnp.zeros((1, LANES), F32)
    for rs, h1, lg in zip(rows, h1s, logits):
        h1_ref[rs, :] = h1
        comb, cnt = _route(lg)
        comb_ref[rs, :] = comb
        counts = counts + cnt
    cnt_ref[0] = jnp.broadcast_to(counts, cnt_ref.shape[1:])


def _mix_out(x2, o_dn, o_sb, g0, b0, wzg, bg, wbdn, wbsb, wout, g1, b1, wr, br, *, tm):
    rows = x2.shape[0]
    const = lambda i: (0, 0)
    row = lambda i: (i, 0)
    full = lambda a: pl.BlockSpec(a.shape, const)
    return pl.pallas_call(
        _mix_out_kernel,
        grid=(rows // tm,),
        in_specs=[
            pl.BlockSpec((tm, D_MODEL), row),
            pl.BlockSpec((tm, DN_WIDTH), row),
            pl.BlockSpec((tm, SB_WIDTH), row),
            full(g0), full(b0), full(wzg), full(bg), full(wbdn), full(wbsb), full(wout), full(g1), full(b1),
            full(wr), full(br),
        ],
        out_specs=[pl.BlockSpec((tm, D_MODEL), row), pl.BlockSpec((tm, LANES), row),
                   pl.BlockSpec((1, SUBLANES, LANES), lambda i: (i, 0, 0))],
        out_shape=[jax.ShapeDtypeStruct((rows, D_MODEL), F32), jax.ShapeDtypeStruct((rows, LANES), F32),
                   jax.ShapeDtypeStruct((rows // tm, SUBLANES, LANES), F32)],
        compiler_params=pltpu.CompilerParams(dimension_semantics=("arbitrary",), vmem_limit_bytes=VMEM_LIMIT),
        name="mix_out",
    )(x2, o_dn, o_sb, g0, b0, wzg, bg, wbdn, wbsb, wout, g1, b1, wr, br)


MOE_TM = 512
SEG_ALIGN = 16
SEG_SIZES = (512, 256, 128, 64, 32, 16)
SORT_ROWS = MOE_TM + 64
X_EXT = D_MODEL + 2 * LANES


def _sorted_one_hot(comb, soff_ref, i):
    tm = comb.shape[0]
    lane = lax.broadcasted_iota(jnp.int32, (tm, LANES), 1)
    onehot = lane == comb[:, 0:1].astype(jnp.int32)
    ri = lax.broadcasted_iota(jnp.int32, (tm, tm), 0)
    ci = lax.broadcasted_iota(jnp.int32, (tm, tm), 1)
    ranks = _dot((ri > ci).astype(BF16), onehot.astype(BF16))
    start = jnp.zeros((1, LANES), F32)
    for g in range(N_GROUPS):
        start = jnp.where(lane[0:1, :] == g, soff_ref[i * N_GROUPS + g].astype(F32), start)
    pos = jnp.sum(jnp.where(onehot, ranks + start, 0.0), -1, keepdims=True)
    slot = lax.broadcasted_iota(jnp.int32, (tm, SORT_ROWS), 1)
    return (slot == pos.astype(jnp.int32)).astype(BF16)


def _for_each_piece(length, fn):
    for size in SEG_SIZES:
        @pl.when((length & size) != 0)
        def _(size=size):
            fn(pl.multiple_of(length & (-2 * size), SEG_ALIGN), size)


def _segment_copies(i, len_ref, src_ref, soff_ref, dst_ref, doff_ref, sem, act):
    for g in range(N_GROUPS):
        n = i * N_GROUPS + g
        so = soff_ref[n]
        do = doff_ref[n]

        def piece(off, size, so=so, do=do):
            act(pltpu.make_async_copy(src_ref.at[pl.ds(pl.multiple_of(so + off, SEG_ALIGN), size)],
                                      dst_ref.at[pl.ds(pl.multiple_of(do + off, SEG_ALIGN), size)], sem))

        _for_each_piece(len_ref[n], piece)


def _dispatch_kernel(len_ref, soff_ref, doff_ref, tlen_ref, toff_ref, nv_ref, h1_ref, comb_ref, xg_ref, xs_ref,
                     zero_ref, sem):
    i = pl.program_id(0)
    comb = comb_ref[...]
    pt = _sorted_one_hot(comb, soff_ref, i)
    lane = lax.broadcasted_iota(jnp.int32, comb.shape, 1)
    c = jnp.where(lane >= ROUTER_COL0, comb, 0.0)
    c_hi = c.astype(BF16)
    c_lo = (c - c_hi.astype(F32)).astype(BF16)
    src = jnp.concatenate([h1_ref[...].astype(BF16), c_hi, c_lo], axis=1)
    xs_ref[...] = _dot_tn(pt, src).astype(BF16)
    to_groups = (i, len_ref, xs_ref, soff_ref, xg_ref, doff_ref, sem)
    _segment_copies(*to_groups, lambda cp: cp.start())

    @pl.when(i == pl.num_programs(0) - 1)
    def _():
        zero_ref[...] = jnp.zeros_like(zero_ref)
        for act in (lambda cp: cp.start(), lambda cp: cp.wait()):
            for g in range(N_GROUPS):
                def piece(off, size, g=g, act=act):
                    act(pltpu.make_async_copy(
                        zero_ref.at[pl.ds(off, size)],
                        xg_ref.at[pl.ds(pl.multiple_of(toff_ref[g] + off, SEG_ALIGN), size)], sem))

                _for_each_piece(tlen_ref[g], piece)

        def zero_tile(k, carry):
            cp = pltpu.make_async_copy(zero_ref, xg_ref.at[pl.ds(pl.multiple_of(k * MOE_TM, MOE_TM), MOE_TM)], sem)
            cp.start()
            cp.wait()
            return carry

        lax.fori_loop(nv_ref[0], xg_ref.shape[0] // MOE_TM, zero_tile, 0)

    _segment_copies(*to_groups, lambda cp: cp.wait())


def _dispatch(h1, comb, tabs, *, n_ffn_tiles):
    rows = h1.shape[0]
    row = lambda i, *_: (i, 0)
    grid_spec = pltpu.PrefetchScalarGridSpec(
        num_scalar_prefetch=len(tabs),
        grid=(rows // MOE_TM,),
        in_specs=[pl.BlockSpec((MOE_TM, D_MODEL), row), pl.BlockSpec((MOE_TM, LANES), row)],
        out_specs=pl.BlockSpec(memory_space=pl.ANY),
        scratch_shapes=[pltpu.VMEM((SORT_ROWS, X_EXT), BF16), pltpu.VMEM((MOE_TM, X_EXT), BF16),
                        pltpu.SemaphoreType.DMA(())],
    )
    return pl.pallas_call(
        _dispatch_kernel,
        grid_spec=grid_spec,
        out_shape=jax.ShapeDtypeStruct((n_ffn_tiles * MOE_TM, X_EXT), BF16),
        compiler_params=pltpu.CompilerParams(dimension_semantics=("arbitrary",), vmem_limit_bytes=VMEM_LIMIT),
        name="moe_dispatch",
    )(*tabs, h1, comb)


def _group_ffn_kernel(tg_ref, nv_ref, x_ref, wg_ref, wu_ref, wd_ref, y_ref):
    k = pl.program_id(0)

    @pl.when(k < nv_ref[0])
    def _():
        g = tg_ref[k]
        xe = x_ref[...]
        hb = xe[:, :D_MODEL]
        comb = xe[:, D_MODEL:D_MODEL + LANES].astype(F32) + xe[:, D_MODEL + LANES:].astype(F32)
        hid = _silu(_dot(hb, wg_ref[0])) * _dot(hb, wu_ref[0])
        lane = lax.broadcasted_iota(jnp.int32, comb.shape, 1)
        parts = []
        for e in range(EXPERTS_PER_GROUP):
            col = ROUTER_COL0 + g * EXPERTS_PER_GROUP + e
            c = jnp.sum(jnp.where(lane == col, comb, 0.0), -1, keepdims=True)
            parts.append(hid[:, e * EXPERT_FF:(e + 1) * EXPERT_FF] * c)
        y_ref[...] = _dot(jnp.concatenate(parts, axis=-1).astype(BF16), wd_ref[0])

    @pl.when(k >= nv_ref[0])
    def _():
        y_ref[...] = jnp.zeros_like(y_ref)


def _group_ffn(xg, tile_group, n_valid, wg, wu, wd):
    n_tiles = xg.shape[0] // MOE_TM
    row = lambda k, tg, nv: (jnp.minimum(k, nv[0] - 1), 0)
    out_row = lambda k, tg, nv: (k, 0)
    grp = lambda k, tg, nv: (tg[k], 0, 0)
    grid_spec = pltpu.PrefetchScalarGridSpec(
        num_scalar_prefetch=2,
        grid=(n_tiles,),
        in_specs=[
            pl.BlockSpec((MOE_TM, X_EXT), row),
            pl.BlockSpec((1,) + wg.shape[1:], grp),
            pl.BlockSpec((1,) + wu.shape[1:], grp),
            pl.BlockSpec((1,) + wd.shape[1:], grp),
        ],
        out_specs=pl.BlockSpec((MOE_TM, D_MODEL), out_row),
    )
    return pl.pallas_call(
        _group_ffn_kernel,
        grid_spec=grid_spec,
        out_shape=jax.ShapeDtypeStruct((n_tiles * MOE_TM, D_MODEL), F32),
        compiler_params=pltpu.CompilerParams(dimension_semantics=("arbitrary",), vmem_limit_bytes=VMEM_LIMIT),
        name="moe_ffn",
    )(tile_group, n_valid, xg, wg, wu, wd)


def _combine_kernel(len_ref, soff_ref, doff_ref, h1_ref, comb_ref, g2_ref, b2_ref, ys_ref, o_ref, ysb_ref, sem):
    i = pl.program_id(0)
    n = pl.num_programs(0)

    def fetch(tile, slot, act):
        _segment_copies(tile, len_ref, ys_ref, doff_ref, ysb_ref.at[slot], soff_ref, sem.at[slot], act)

    def start_fetch(tile, slot):
        ysb_ref[slot] = jnp.zeros(ysb_ref.shape[1:], F32)
        fetch(tile, slot, lambda cp: cp.start())

    @pl.when(i == 0)
    def _():
        start_fetch(0, 0)

    for slot in range(2):
        @pl.when((i + 1 < n) & ((i + 1) % 2 == slot))
        def _(slot=slot):
            start_fetch(i + 1, slot)

    pt = _sorted_one_hot(comb_ref[...], soff_ref, i)
    for slot in range(2):
        @pl.when(i % 2 == slot)
        def _(slot=slot):
            fetch(i, slot, lambda cp: cp.wait())

    ys = ysb_ref[i % 2]
    hi = ys.astype(BF16)
    lo = (ys - hi.astype(F32)).astype(BF16)
    ffn = _dot(pt, hi) + _dot(pt, lo)
    o_ref[...] = _layer_norm(DEEPNORM_ALPHA * h1_ref[...] + ffn, g2_ref[...], b2_ref[...])


def _combine(h1, comb, ys, g2, b2, tabs):
    rows = h1.shape[0]
    row = lambda i, *_: (i, 0)
    const = lambda i, *_: (0, 0)
    grid_spec = pltpu.PrefetchScalarGridSpec(
        num_scalar_prefetch=len(tabs),
        grid=(rows // MOE_TM,),
        in_specs=[pl.BlockSpec((MOE_TM, D_MODEL), row), pl.BlockSpec((MOE_TM, LANES), row),
                  pl.BlockSpec((1, D_MODEL), const), pl.BlockSpec((1, D_MODEL), const),
                  pl.BlockSpec(memory_space=pl.ANY)],
        out_specs=pl.BlockSpec((MOE_TM, D_MODEL), row),
        scratch_shapes=[pltpu.VMEM((2, SORT_ROWS, D_MODEL), F32), pltpu.SemaphoreType.DMA((2,))],
    )
    return pl.pallas_call(
        _combine_kernel,
        grid_spec=grid_spec,
        out_shape=jax.ShapeDtypeStruct((rows, D_MODEL), F32),
        compiler_params=pltpu.CompilerParams(dimension_semantics=("arbitrary",), vmem_limit_bytes=VMEM_LIMIT),
        name="moe_combine",
    )(*tabs, h1, comb, g2, b2, ys)


def _moe_tables(cnt, rows):
    n_tiles = cnt.shape[0]
    up = lambda a, m: (a + m - 1) // m * m
    seg = up(cnt, SEG_ALIGN)
    soff = jnp.cumsum(seg, axis=1) - seg
    gtot = jnp.sum(seg, axis=0)
    gpad = up(gtot, MOE_TM)
    gbase = jnp.cumsum(gpad) - gpad
    doff = gbase[None, :] + jnp.cumsum(seg, axis=0) - seg
    n_ffn_tiles = -(-(rows + n_tiles * N_GROUPS * (SEG_ALIGN - 1)) // MOE_TM) + N_GROUPS
    starts = jnp.arange(n_ffn_tiles, dtype=jnp.int32) * MOE_TM
    tile_group = jnp.minimum(jnp.sum(starts[:, None] >= (gbase + gpad)[None, :], axis=1), N_GROUPS - 1)
    n_valid = (jnp.sum(gpad) // MOE_TM).reshape(1)
    i32 = lambda a: a.reshape(-1).astype(jnp.int32)
    return dict(seg=i32(seg), soff=i32(soff), doff=i32(doff), tlen=i32(gpad - gtot), toff=i32(gbase + gtot),
                tile_group=i32(tile_group), n_valid=i32(n_valid)), n_ffn_tiles


def _pad_lanes(a, col0=0):
    return jnp.pad(a, ((0, 0), (col0, LANES - col0 - a.shape[1])))


def kernel(x, meta_tokens, ln_emb_g, ln_emb_b, w_in, b_gate, dn_conv_w, dn_a_log, dn_dt_bias, dn_norm_g,
           w_branch_dn, w_branch_sb, w_out, ln1_g, ln1_b, router_group_w, router_group_b, router_expert_w,
           router_expert_b, expert_w_gate, expert_w_up, expert_w_down, ln2_g, ln2_b):
    batch, seq, d = x.shape
    assert d == D_MODEL and w_in.shape[0] == 1 and seq % max(GDN_BLOCK, SB_T) == 0 and batch % GDN_STREAMS == 0
    rows = batch * seq
    tm = 512
    assert rows % tm == 0
    row1 = lambda a: a.reshape(1, -1).astype(F32)

    w = w_in[0]
    c0 = 3 * DN_WIDTH
    c1 = c0 + DN_WIDTH
    c2 = c1 + 2 * DN_HEADS
    c3 = c2 + 3 * SB_WIDTH
    w_dn = w[:, :c0].astype(BF16)
    w_ba = _pad_lanes(w[:, c1:c2]).astype(BF16)
    w_sb = w[:, c2:c3].astype(BF16)
    w_zg = jnp.concatenate([w[:, c0:c1], w[:, c3:]], axis=1).astype(BF16)

    x2 = x.reshape(rows, d)
    g0, b0 = row1(ln_emb_g), row1(ln_emb_b)
    dn, sb, ba = _ln_proj(x2, g0, b0, w_dn, w_sb, w_ba, tm=tm)
    xm = jnp.concatenate([jnp.zeros((META_PAD, d), x.dtype), meta_tokens.astype(x.dtype)], axis=0)
    mdn, msb, mba = _ln_proj(xm, g0, b0, w_dn, w_sb, w_ba, tm=META_ROWS, n_zero=META_PAD)

    alog_row = _pad_lanes(dn_a_log[0].reshape(1, -1).astype(F32), DN_HEADS)
    dtb_row = _pad_lanes(dn_dt_bias[0].reshape(1, -1).astype(F32), DN_HEADS)
    o_dn = _gdn(dn, ba, mdn, mba, dn_conv_w[0].astype(F32), alog_row, dtb_row, row1(dn_norm_g[0]),
                batch=batch, seq=seq)
    o_sb = _sb_attn(sb, msb, batch=batch, seq=seq)

    w_r = _pad_lanes(jnp.concatenate(
        [router_group_w[0], router_expert_w[0].transpose(1, 0, 2).reshape(d, N_EXPERTS)], axis=1)).astype(BF16)
    b_r = _pad_lanes(jnp.concatenate(
        [router_group_b[0].reshape(1, -1), router_expert_b[0].reshape(1, -1)], axis=1).astype(F32))
    h1, comb, cnt = _mix_out(x2, o_dn, o_sb, g0, b0, w_zg, b_gate[0].astype(F32), w_branch_dn[0].astype(BF16),
                             w_branch_sb[0].astype(BF16), w_out[0].astype(BF16), row1(ln1_g[0]),
                             row1(ln1_b[0]), w_r, b_r, tm=MOE_TM)

    ef = EXPERTS_PER_GROUP * EXPERT_FF
    wg = expert_w_gate[0].transpose(0, 2, 1, 3).reshape(N_GROUPS, d, ef).astype(BF16)
    wu = expert_w_up[0].transpose(0, 2, 1, 3).reshape(N_GROUPS, d, ef).astype(BF16)
    wd = expert_w_down[0].reshape(N_GROUPS, ef, d).astype(BF16)
    tabs, n_ffn_tiles = _moe_tables(cnt[:, 0, :N_GROUPS].astype(jnp.int32), rows)
    seg_tabs = (tabs["seg"], tabs["soff"], tabs["doff"])
    xg = _dispatch(h1, comb, seg_tabs + (tabs["tlen"], tabs["toff"], tabs["n_valid"]), n_ffn_tiles=n_ffn_tiles)
    ys = _group_ffn(xg, tabs["tile_group"], tabs["n_valid"], wg, wu, wd)
    out = _combine(h1, comb, ys, row1(ln2_g[0]), row1(ln2_b[0]), seg_tabs)
    return out.reshape(batch, seq, d)
```

```python
import functools

import jax
import jax.numpy as jnp
import numpy as np
from jax import lax
from jax.experimental import pallas as pl
from jax.experimental.pallas import tpu as pltpu

F32 = jnp.float32
BF16 = jnp.bfloat16

D_MODEL = 1024
N_META = 16
DN_HEADS = 4
DN_HEAD_DIM = 128
DN_WIDTH = DN_HEADS * DN_HEAD_DIM
DN_CONV = 4
DN_CHUNK = 64
SB_HEADS = 8
SB_HEAD_DIM = 64
SB_WIDTH = SB_HEADS * SB_HEAD_DIM
N_GROUPS = 4
EXPERTS_PER_GROUP = 8
N_EXPERTS = N_GROUPS * EXPERTS_PER_GROUP
EXPERT_FF = 256
DEEPNORM_ALPHA = 2.0 ** 0.25
LN_EPS = 1e-5
RMS_EPS = 1e-6

LANES = 128
SUBLANES = 8
META_ROWS = DN_CHUNK
META_PAD = META_ROWS - N_META
ROUTER_COL0 = N_GROUPS
VMEM_LIMIT = 56 * 1024 * 1024


def _layer_norm(x, g, b):
    mu = jnp.mean(x, -1, keepdims=True)
    xc = x - mu
    var = jnp.mean(xc * xc, -1, keepdims=True)
    return xc * lax.rsqrt(var + LN_EPS) * g + b


def _sigmoid(x):
    return 1.0 / (1.0 + jnp.exp(-x))


def _softplus(x):
    return jnp.maximum(x, 0.0) + jnp.log(1.0 + jnp.exp(-jnp.abs(x)))


def _silu(x):
    return x * _sigmoid(x)


def _dot(a, b):
    return jnp.dot(a, b, preferred_element_type=F32)


def _dot_nt(a, b):
    return lax.dot_general(a, b, (((1,), (1,)), ((), ())), preferred_element_type=F32)


def _dot_tn(a, b):
    return lax.dot_general(a, b, (((0,), (0,)), ((), ())), preferred_element_type=F32)


def _ln_proj_kernel(x_ref, g_ref, b_ref, wdn_ref, wsb_ref, wba_ref, dn_ref, sb_ref, ba_ref, *, n_zero):
    h = _layer_norm(x_ref[...], g_ref[...], b_ref[...])
    if n_zero:
        rows = lax.broadcasted_iota(jnp.int32, h.shape, 0)
        h = jnp.where(rows >= n_zero, h, 0.0)
    hb = h.astype(BF16)
    dn_ref[...] = _dot(hb, wdn_ref[...])
    sb_ref[...] = _dot(hb, wsb_ref[...]).astype(BF16)
    ba_ref[...] = _dot(hb, wba_ref[...])


def _ln_proj(x2, g, b, wdn, wsb, wba, *, tm, n_zero=0):
    rows = x2.shape[0]
    const = lambda i: (0, 0)
    row = lambda i: (i, 0)
    return pl.pallas_call(
        functools.partial(_ln_proj_kernel, n_zero=n_zero),
        grid=(rows // tm,),
        in_specs=[
            pl.BlockSpec((tm, D_MODEL), row),
            pl.BlockSpec((1, D_MODEL), const),
            pl.BlockSpec((1, D_MODEL), const),
            pl.BlockSpec(wdn.shape, const),
            pl.BlockSpec(wsb.shape, const),
            pl.BlockSpec(wba.shape, const),
        ],
        out_specs=[
            pl.BlockSpec((tm, 3 * DN_WIDTH), row),
            pl.BlockSpec((tm, 3 * SB_WIDTH), row),
            pl.BlockSpec((tm, LANES), row),
        ],
        out_shape=[
            jax.ShapeDtypeStruct((rows, 3 * DN_WIDTH), F32),
            jax.ShapeDtypeStruct((rows, 3 * SB_WIDTH), BF16),
            jax.ShapeDtypeStruct((rows, LANES), F32),
        ],
        compiler_params=pltpu.CompilerParams(dimension_semantics=("arbitrary",), vmem_limit_bytes=VMEM_LIMIT),
        name="ln_proj",
    )(x2, g, b, wdn, wsb, wba)


GDN_BLOCK = 256
GDN_STREAMS = 1
GDN_CHUNK = DN_CHUNK
CONV_HIST = 8


def _gdn_rows(streams, n, n_zero, cw_ref, alog_ref, dtb_ref, ng_ref):
    c = min(n, GDN_CHUNK)
    ri = lax.broadcasted_iota(jnp.int32, (n, n), 0)
    ci = lax.broadcasted_iota(jnp.int32, (n, n), 1)
    same = (ri // c) == (ci // c)
    causal = same & (ri >= ci)
    strict = same & (ri > ci)
    tril = causal.astype(BF16)

    pre = []
    for src_ref, ba_ref, _, xbuf, _, vn_ref in streams:
        xbuf[CONV_HIST:CONV_HIST + n, :] = src_ref[...]
        acc = xbuf[CONV_HIST:CONV_HIST + n, :] * cw_ref[DN_CONV - 1:DN_CONV, :]
        for i in range(DN_CONV - 1):
            s = DN_CONV - 1 - i
            acc = acc + xbuf[CONV_HIST - s:CONV_HIST - s + n, :] * cw_ref[i:i + 1, :]
        hist = xbuf[n:n + CONV_HIST, :]
        xbuf[0:CONV_HIST, :] = hist
        qkv = _silu(acc)

        ba = ba_ref[...]
        beta_all = _sigmoid(ba)
        g_all = -jnp.exp(alog_ref[...]) * _softplus(ba + dtb_ref[...])
        if n_zero:
            rows = lax.broadcasted_iota(jnp.int32, g_all.shape, 0)
            g_all = jnp.where(rows >= n_zero, g_all, 0.0)
        g_hi = g_all.astype(BF16)
        g_lo = (g_all - g_hi.astype(F32)).astype(BF16)
        dec = _dot(tril, g_hi) + _dot(tril, g_lo)
        vn_ref[...] = jnp.zeros_like(vn_ref)
        pre.append((qkv, beta_all, dec, dec.T))

    units = [(si, h) for si in range(len(streams)) for h in range(DN_HEADS)]
    qe, ks, d_cols, qks, rhss, ps = [], [], [], [], [], []
    for si, h in units:
        qkv, beta_all, dec, dec_t = pre[si]
        q = qkv[:, h * DN_HEAD_DIM:(h + 1) * DN_HEAD_DIM]
        k = qkv[:, DN_WIDTH + h * DN_HEAD_DIM:DN_WIDTH + (h + 1) * DN_HEAD_DIM]
        v = qkv[:, 2 * DN_WIDTH + h * DN_HEAD_DIM:2 * DN_WIDTH + (h + 1) * DN_HEAD_DIM]
        q = q * lax.rsqrt(jnp.sum(q * q, -1, keepdims=True) + RMS_EPS) * (DN_HEAD_DIM ** -0.5)
        k = k * lax.rsqrt(jnp.sum(k * k, -1, keepdims=True) + RMS_EPS)
        beta = beta_all[:, h:h + 1]
        d_col = dec[:, DN_HEADS + h:DN_HEADS + h + 1]
        d_row = dec_t[DN_HEADS + h:DN_HEADS + h + 1, :]
        lmask = jnp.where(causal, jnp.exp(jnp.where(causal, d_col - d_row, 0.0)), 0.0)
        kb = k * beta
        k16 = k.astype(BF16)
        a = jnp.where(strict, _dot_nt(kb.astype(BF16), k16) * lmask, 0.0)
        qks.append((_dot_nt(q.astype(BF16), k16) * lmask).astype(BF16))
        e_col = jnp.exp(d_col)
        rhss.append(jnp.concatenate([v * beta, kb * e_col], axis=1))
        qe.append(q * e_col)
        ks.append(k)
        d_cols.append(d_col)
        ps.append(-a)
    ts = list(ps)
    for _ in range(c.bit_length() - 2):
        p16s = [p.astype(BF16) for p in ps]
        ps = [_dot(p16, p16) for p16 in p16s]
        ts = [t + p + _dot(t.astype(BF16), p.astype(BF16)) for t, p in zip(ts, ps)]
    uws = [rhs + _dot(t.astype(BF16), rhs.astype(BF16)) for t, rhs in zip(ts, rhss)]

    for ch in range(n // c):
        r0 = ch * c
        for u, (si, h) in enumerate(units):
            _, _, o_ref, _, s_ref, vn_ref = streams[si]
            k, d_col, qk, uw = ks[u], d_cols[u], qks[u], uws[u]
            d_c = d_col[r0:r0 + c, :]
            d_last = d_col[r0 + c - 1:r0 + c, :]
            s = s_ref[h]
            wq = jnp.concatenate([uw[r0:r0 + c, DN_HEAD_DIM:], qe[u][r0:r0 + c, :]], axis=0)
            r = _dot(wq.astype(BF16), s.astype(BF16))
            v_new = uw[r0:r0 + c, :DN_HEAD_DIM] - r[:c, :]
            vn16 = v_new.astype(BF16)
            vn_ref[h, r0:r0 + c, :] = vn16
            o = r[c:, :] + _dot(qk[r0:r0 + c, :], vn_ref[h])
            k_dec = k[r0:r0 + c, :] * jnp.exp(d_last - d_c)
            s_ref[h] = s * jnp.exp(d_last) + _dot_tn(k_dec.astype(BF16), vn16)
            if o_ref is not None:
                o_n = o * lax.rsqrt(jnp.mean(o * o, -1, keepdims=True) + RMS_EPS) * ng_ref[...]
                o_ref[r0:r0 + c, h * DN_HEAD_DIM:(h + 1) * DN_HEAD_DIM] = o_n


def _gdn_kernel(dn_ref, ba_ref, mdn_ref, mba_ref, cw_ref, alog_ref, dtb_ref, ng_ref, o_ref, xbuf, s_ref, vn_ref,
                vnm_ref):
    @pl.when(pl.program_id(1) == 0)
    def _():
        s_ref[0] = jnp.zeros(s_ref.shape[1:], F32)
        xbuf[0, 0:CONV_HIST, :] = jnp.zeros((CONV_HIST, xbuf.shape[2]), F32)
        _gdn_rows([(mdn_ref, mba_ref, None, xbuf.at[0], s_ref.at[0], vnm_ref)], META_ROWS, META_PAD,
                  cw_ref, alog_ref, dtb_ref, ng_ref)
        for si in range(1, GDN_STREAMS):
            s_ref[si] = s_ref[0]
            xbuf[si, 0:CONV_HIST, :] = xbuf[0, 0:CONV_HIST, :]

    _gdn_rows([(dn_ref.at[si], ba_ref.at[si], o_ref.at[si], xbuf.at[si], s_ref.at[si], vn_ref.at[si])
               for si in range(GDN_STREAMS)], GDN_BLOCK, 0, cw_ref, alog_ref, dtb_ref, ng_ref)


def _gdn(dn, ba, mdn, mba, conv_w, alog_row, dtb_row, norm_g, *, batch, seq):
    nb = seq // GDN_BLOCK
    const = lambda b, j: (0, 0)
    row = lambda b, j: (b, j, 0)
    dn = dn.reshape(batch, seq, dn.shape[-1])
    ba = ba.reshape(batch, seq, ba.shape[-1])
    out = pl.pallas_call(
        _gdn_kernel,
        grid=(batch // GDN_STREAMS, nb),
        in_specs=[
            pl.BlockSpec((GDN_STREAMS, GDN_BLOCK, 3 * DN_WIDTH), row),
            pl.BlockSpec((GDN_STREAMS, GDN_BLOCK, LANES), row),
            pl.BlockSpec((META_ROWS, 3 * DN_WIDTH), const),
            pl.BlockSpec((META_ROWS, LANES), const),
            pl.BlockSpec((DN_CONV, 3 * DN_WIDTH), const),
            pl.BlockSpec((1, LANES), const),
            pl.BlockSpec((1, LANES), const),
            pl.BlockSpec((1, DN_HEAD_DIM), const),
        ],
        out_specs=pl.BlockSpec((GDN_STREAMS, GDN_BLOCK, DN_WIDTH), row),
        out_shape=jax.ShapeDtypeStruct((batch, seq, DN_WIDTH), F32),
        scratch_shapes=[
            pltpu.VMEM((GDN_STREAMS, GDN_BLOCK + CONV_HIST, 3 * DN_WIDTH), F32),
            pltpu.VMEM((GDN_STREAMS, DN_HEADS, DN_HEAD_DIM, DN_HEAD_DIM), F32),
            pltpu.VMEM((GDN_STREAMS, DN_HEADS, GDN_BLOCK, DN_HEAD_DIM), BF16),
            pltpu.VMEM((DN_HEADS, META_ROWS, DN_HEAD_DIM), BF16),
        ],
        compiler_params=pltpu.CompilerParams(dimension_semantics=("arbitrary", "arbitrary"),
                                             vmem_limit_bytes=VMEM_LIMIT),
        name="gdn",
    )(dn, ba, mdn, mba, conv_w, alog_row, dtb_row, norm_g)
    return out.reshape(batch * seq, DN_WIDTH)


SB_T = 2 * LANES
HEADS_PER_BLOCK = LANES // SB_HEAD_DIM
SB_SLOTS = 4
SB_UNROLL = 16
NEG_BIG = -1e30


def _sb_schedule(nq):
    qoff, koff, bsel, first = [], [], [], []
    for qi in range(nq):
        for kj in range(qi, -1, -1):
            qoff.append(qi * SB_T)
            koff.append(kj * SB_T)
            bsel.append(1 if kj == qi else 0)
            first.append(1 if kj == qi else 0)
    return [np.asarray(a, np.int32) for a in (qoff, koff, bsel, first)]


def _sb_kernel(qoff_ref, koff_ref, bsel_ref, first_ref, q_ref, k_ref, v_ref, mk_ref, mv_ref, o_ref,
               qm_s, bias_s, nu_s, z_s, sp_s, later_s, w_s, acc_s, carry_s, accq_s, carryq_s, *, n_tiles):
    t = SB_T
    ri = lax.broadcasted_iota(jnp.int32, (t, t), 0)
    ci = lax.broadcasted_iota(jnp.int32, (t, t), 1)
    bias_s[0] = jnp.zeros((t, t), F32)
    bias_s[1] = jnp.where(ci < ri, 0.0, NEG_BIG)
    acc_s[...] = jnp.zeros_like(acc_s)
    carry_s[...] = jnp.zeros_like(carry_s)
    nu_s[...] = jnp.where(ri > ci, -1.0, 0.0).astype(BF16)
    lane = lax.broadcasted_iota(jnp.int32, (t, LANES), 1)
    scale = SB_HEAD_DIM ** -0.5

    q_all = q_ref[...]
    lane_q = lax.broadcasted_iota(jnp.int32, q_all.shape, 1)
    for h in range(HEADS_PER_BLOCK):
        qm_s[h] = jnp.where((lane_q // SB_HEAD_DIM) == h, q_all, jnp.zeros_like(q_all)) * scale

    def st_logits(n, s):
        qo = pl.multiple_of(qoff_ref[n], t)
        ko = pl.multiple_of(koff_ref[n], t)
        kt = k_ref[pl.ds(ko, t), :]
        bias = bias_s[bsel_ref[n]]
        for h in range(HEADS_PER_BLOCK):
            z_s[s, h] = _dot_nt(qm_s[h, pl.ds(qo, t), :], kt) + bias

    def st_softplus(n, s):
        for h in range(HEADS_PER_BLOCK):
            z = z_s[s, h]
            sp = jnp.maximum(z, 0.0) + jnp.log(1.0 + jnp.exp(-jnp.abs(z)))
            sp_s[s, h] = sp.astype(BF16)
            z_s[s, h] = z - sp

    def st_keysum(n, s):
        for h in range(HEADS_PER_BLOCK):
            later_s[s, h] = _dot(sp_s[s, h], nu_s[...])

    def st_weights(n, s):
        keep = jnp.where(first_ref[n] == 1, 0.0, 1.0)
        qb = qoff_ref[n] // t
        for h in range(HEADS_PER_BLOCK):
            later = later_s[s, h]
            carry = carry_s[h] * keep
            logw = z_s[s, h] + later + jnp.concatenate([carry] * (t // LANES), axis=1)
            w_s[s, h] = jnp.exp(logw).astype(BF16)
            total = later[:, 0:1] - sp_s[s, h, :, 0:1].astype(F32)
            carry = carry + jnp.broadcast_to(total, (t, LANES))
            carry_s[h] = carry
            carryq_s[qb, h] = carry

    def st_values(n, s):
        ko = pl.multiple_of(koff_ref[n], t)
        vt = v_ref[pl.ds(ko, t), :]
        keep = jnp.where(first_ref[n] == 1, 0.0, 1.0)
        qb = qoff_ref[n] // t
        for h in range(HEADS_PER_BLOCK):
            acc = acc_s[h] * keep + _dot(w_s[s, h], vt)
            acc_s[h] = acc
            accq_s[qb, h] = acc

    stages = (st_logits, st_softplus, st_keysum, st_weights, st_values)
    depth = len(stages)

    def trip(it, parity):
        for d in reversed(range(depth)):
            n = it - d
            if isinstance(n, int) and not 0 <= n < n_tiles:
                continue
            stages[d](n, (parity - d) % SB_SLOTS)

    for it in range(depth - 1):
        trip(it, it % SB_SLOTS)
    first_full = depth - 1
    n_full = n_tiles - first_full
    n_loop = n_full // SB_UNROLL

    def body(u, c):
        base = first_full + u * SB_UNROLL
        for j in range(SB_UNROLL):
            trip(base + j, (first_full + j) % SB_SLOTS)
        return c

    lax.fori_loop(0, n_loop, body, 0)
    for it in range(first_full + n_loop * SB_UNROLL, n_tiles + depth - 1):
        trip(it, it % SB_SLOTS)

    m = META_ROWS
    mrow = lax.broadcasted_iota(jnp.int32, (HEADS_PER_BLOCK * m, LANES), 0)
    mlane = lax.broadcasted_iota(jnp.int32, (HEADS_PER_BLOCK * m, LANES), 1)
    own = (mrow // m) == (mlane // SB_HEAD_DIM)
    mkk = jnp.where(own, jnp.concatenate([mk_ref[...]] * HEADS_PER_BLOCK, axis=0), jnp.zeros((), BF16))
    mvv = jnp.where(own, jnp.concatenate([mv_ref[...]] * HEADS_PER_BLOCK, axis=0), jnp.zeros((), BF16))
    mr = lax.broadcasted_iota(jnp.int32, (LANES, LANES), 0)
    mc = lax.broadcasted_iota(jnp.int32, (LANES, LANES), 1)
    nu_m = jnp.where((mr > mc) & ((mr // m) == (mc // m)), -1.0, 0.0).astype(BF16)
    mbias = jnp.where((lane % m) >= META_PAD, 0.0, NEG_BIG)
    head1 = (lane // SB_HEAD_DIM) == 1
    for qb in range(q_ref.shape[0] // t):
        rows = slice(qb * t, (qb + 1) * t)
        z = _dot_nt(q_ref[rows, :] * scale, mkk) + mbias
        sp = jnp.maximum(z, 0.0) + jnp.log(1.0 + jnp.exp(-jnp.abs(z)))
        later = _dot(sp.astype(BF16), nu_m)
        carry = jnp.where(head1, carryq_s[qb, 1], carryq_s[qb, 0])
        w = jnp.exp(z - sp + later + carry)
        acc = jnp.where(head1, accq_s[qb, 1], accq_s[qb, 0]) + _dot(w.astype(BF16), mvv)
        o_ref[rows, :] = acc.astype(o_ref.dtype)


def _sb_attn(sb, msb, *, batch, seq):
    nq = seq // SB_T
    n_hb = SB_WIDTH // LANES
    sched = _sb_schedule(nq)
    n_tiles = int(sched[0].shape[0])
    grid_spec = pltpu.PrefetchScalarGridSpec(
        num_scalar_prefetch=len(sched),
        grid=(batch, n_hb),
        in_specs=[
            pl.BlockSpec((seq, LANES), lambda b, hp, *_: (b, hp)),
            pl.BlockSpec((seq, LANES), lambda b, hp, *_: (b, n_hb + hp)),
            pl.BlockSpec((seq, LANES), lambda b, hp, *_: (b, 2 * n_hb + hp)),
            pl.BlockSpec((META_ROWS, LANES), lambda b, hp, *_: (0, n_hb + hp)),
            pl.BlockSpec((META_ROWS, LANES), lambda b, hp, *_: (0, 2 * n_hb + hp)),
        ],
        out_specs=pl.BlockSpec((seq, LANES), lambda b, hp, *_: (b, hp)),
        scratch_shapes=[
            pltpu.VMEM((HEADS_PER_BLOCK, seq, LANES), BF16),
            pltpu.VMEM((2, SB_T, SB_T), F32),
            pltpu.VMEM((SB_T, SB_T), BF16),
            pltpu.VMEM((SB_SLOTS, HEADS_PER_BLOCK, SB_T, SB_T), F32),
            pltpu.VMEM((SB_SLOTS, HEADS_PER_BLOCK, SB_T, SB_T), BF16),
            pltpu.VMEM((SB_SLOTS, HEADS_PER_BLOCK, SB_T, SB_T), F32),
            pltpu.VMEM((SB_SLOTS, HEADS_PER_BLOCK, SB_T, SB_T), BF16),
            pltpu.VMEM((HEADS_PER_BLOCK, SB_T, LANES), F32),
            pltpu.VMEM((HEADS_PER_BLOCK, SB_T, LANES), F32),
            pltpu.VMEM((nq, HEADS_PER_BLOCK, SB_T, LANES), F32),
            pltpu.VMEM((nq, HEADS_PER_BLOCK, SB_T, LANES), F32),
        ],
    )
    return pl.pallas_call(
        functools.partial(_sb_kernel, n_tiles=n_tiles),
        grid_spec=grid_spec,
        out_shape=jax.ShapeDtypeStruct((batch * seq, SB_WIDTH), BF16),
        compiler_params=pltpu.CompilerParams(dimension_semantics=("arbitrary", "arbitrary"),
                                             vmem_limit_bytes=VMEM_LIMIT),
        name="sb_attn",
    )(*[jnp.asarray(a) for a in sched], sb, sb, sb, msb, msb)


MIX_CHAINS = 2


def _masked_lane_max(x, mask):
    return jnp.max(jnp.where(mask, x, -jnp.inf), -1, keepdims=True)


def _first_lane_eq(x, val, mask, lane):
    return jnp.min(jnp.where(mask & (x == val), lane, LANES), -1, keepdims=True)


def _route(logits):
    lane = lax.broadcasted_iota(jnp.int32, logits.shape, 1)
    gmask = lane < N_GROUPS
    gmax = _masked_lane_max(logits, gmask)
    g_idx = _first_lane_eq(logits, gmax, gmask, lane)
    g_prob = 1.0 / jnp.sum(jnp.where(gmask, jnp.exp(logits - gmax), 0.0), -1, keepdims=True)
    lo = ROUTER_COL0 + g_idx * EXPERTS_PER_GROUP
    emask = (lane >= lo) & (lane < lo + EXPERTS_PER_GROUP)
    t1 = _masked_lane_max(logits, emask)
    i1 = _first_lane_eq(logits, t1, emask, lane)
    emask2 = emask & (lane != i1)
    t2 = _masked_lane_max(logits, emask2)
    i2 = _first_lane_eq(logits, t2, emask2, lane)
    e = jnp.exp(t2 - t1)
    w1 = g_prob / (1.0 + e)
    w2 = g_prob * e / (1.0 + e)
    comb = jnp.where(lane == i1, w1, 0.0) + jnp.where(lane == i2, w2, 0.0)
    comb = jnp.where(lane == 0, g_idx.astype(F32), comb)
    counts = jnp.sum((lane == g_idx).astype(F32), 0, keepdims=True)
    return comb, counts


def _mix_out_kernel(x_ref, odn_ref, osb_ref, g0_ref, b0_ref, wzg_ref, bg_ref, wbdn_ref, wbsb_ref, wout_ref,
                    g1_ref, b1_ref, wr_ref, br_ref, h1_ref, comb_ref, cnt_ref):
    sub = x_ref.shape[0] // MIX_CHAINS
    rows = [slice(r0, r0 + sub) for r0 in range(0, x_ref.shape[0], sub)]
    h0s = [_layer_norm(x_ref[rs, :], g0_ref[...], b0_ref[...]) for rs in rows]
    zgs = [_dot(h0.astype(BF16), wzg_ref[...]) for h0 in h0s]
    bs = [_dot(osb_ref[rs, :], wbsb_ref[...]) for rs in rows]
    o_dns = [odn_ref[rs, :] * _silu(zg[:, :DN_WIDTH]) for rs, zg in zip(rows, zgs)]
    as_ = [_dot(o_dn.astype(BF16), wbdn_ref[...]) for o_dn in o_dns]
    merged = [_sigmoid(zg[:, DN_WIDTH:DN_WIDTH + D_MODEL] + bg_ref[0:1, :]) * a
              + _sigmoid(zg[:, DN_WIDTH + D_MODEL:] + bg_ref[1:2, :]) * b for zg, a, b in zip(zgs, as_, bs)]
    mixes = [_dot(m.astype(BF16), wout_ref[...]) for m in merged]
    h1s = [_layer_norm(DEEPNORM_ALPHA * h0 + mix, g1_ref[...], b1_ref[...]) for h0, mix in zip(h0s, mixes)]
    logits = [_dot(h1.astype(BF16), wr_ref[...]) + br_ref[...] for h1 in h1s]
    counts = jnp.zeros((1, LANES), F32)
    for rs, h1, lg in zip(rows, h1s, logits):
        h1_ref[rs, :] = h1
        comb, cnt = _route(lg)
        comb_ref[rs, :] = comb
        counts = counts + cnt
    cnt_ref[0] = jnp.broadcast_to(counts, cnt_ref.shape[1:])


def _mix_out(x2, o_dn, o_sb, g0, b0, wzg, bg, wbdn, wbsb, wout, g1, b1, wr, br, *, tm):
    rows = x2.shape[0]
    const = lambda i: (0, 0)
    row = lambda i: (i, 0)
    full = lambda a: pl.BlockSpec(a.shape, const)
    return pl.pallas_call(
        _mix_out_kernel,
        grid=(rows // tm,),
        in_specs=[
            pl.BlockSpec((tm, D_MODEL), row),
            pl.BlockSpec((tm, DN_WIDTH), row),
            pl.BlockSpec((tm, SB_WIDTH), row),
            full(g0), full(b0), full(wzg), full(bg), full(wbdn), full(wbsb), full(wout), full(g1), full(b1),
            full(wr), full(br),
        ],
        out_specs=[pl.BlockSpec((tm, D_MODEL), row), pl.BlockSpec((tm, LANES), row),
                   pl.BlockSpec((1, SUBLANES, LANES), lambda i: (i, 0, 0))],
        out_shape=[jax.ShapeDtypeStruct((rows, D_MODEL), F32), jax.ShapeDtypeStruct((rows, LANES), F32),
                   jax.ShapeDtypeStruct((rows // tm, SUBLANES, LANES), F32)],
        compiler_params=pltpu.CompilerParams(dimension_semantics=("arbitrary",), vmem_limit_bytes=VMEM_LIMIT),
        name="mix_out",
    )(x2, o_dn, o_sb, g0, b0, wzg, bg, wbdn, wbsb, wout, g1, b1, wr, br)


MOE_TM = 512
SEG_ALIGN = 16
SEG_SIZES = (512, 256, 128, 64, 32, 16)
SORT_ROWS = MOE_TM + 64
X_EXT = D_MODEL + 2 * LANES


def _sorted_one_hot(comb, soff_ref, i):
    tm = comb.shape[0]
    lane = lax.broadcasted_iota(jnp.int32, (tm, LANES), 1)
    onehot = lane == comb[:, 0:1].astype(jnp.int32)
    ri = lax.broadcasted_iota(jnp.int32, (tm, tm), 0)
    ci = lax.broadcasted_iota(jnp.int32, (tm, tm), 1)
    ranks = _dot((ri > ci).astype(BF16), onehot.astype(BF16))
    start = jnp.zeros((1, LANES), F32)
    for g in range(N_GROUPS):
        start = jnp.where(lane[0:1, :] == g, soff_ref[i * N_GROUPS + g].astype(F32), start)
    pos = jnp.sum(jnp.where(onehot, ranks + start, 0.0), -1, keepdims=True)
    slot = lax.broadcasted_iota(jnp.int32, (tm, SORT_ROWS), 1)
    return (slot == pos.astype(jnp.int32)).astype(BF16)


def _for_each_piece(length, fn):
    for size in SEG_SIZES:
        @pl.when((length & size) != 0)
        def _(size=size):
            fn(pl.multiple_of(length & (-2 * size), SEG_ALIGN), size)


def _segment_copies(i, len_ref, src_ref, soff_ref, dst_ref, doff_ref, sem, act):
    for g in range(N_GROUPS):
        n = i * N_GROUPS + g
        so = soff_ref[n]
        do = doff_ref[n]

        def piece(off, size, so=so, do=do):
            act(pltpu.make_async_copy(src_ref.at[pl.ds(pl.multiple_of(so + off, SEG_ALIGN), size)],
                                      dst_ref.at[pl.ds(pl.multiple_of(do + off, SEG_ALIGN), size)], sem))

        _for_each_piece(len_ref[n], piece)


def _dispatch_kernel(len_ref, soff_ref, doff_ref, tlen_ref, toff_ref, nv_ref, h1_ref, comb_ref, xg_ref, xs_ref,
                     zero_ref, sem):
    i = pl.program_id(0)
    comb = comb_ref[...]
    pt = _sorted_one_hot(comb, soff_ref, i)
    lane = lax.broadcasted_iota(jnp.int32, comb.shape, 1)
    c = jnp.where(lane >= ROUTER_COL0, comb, 0.0)
    c_hi = c.astype(BF16)
    c_lo = (c - c_hi.astype(F32)).astype(BF16)
    src = jnp.concatenate([h1_ref[...].astype(BF16), c_hi, c_lo], axis=1)
    xs_ref[...] = _dot_tn(pt, src).astype(BF16)
    to_groups = (i, len_ref, xs_ref, soff_ref, xg_ref, doff_ref, sem)
    _segment_copies(*to_groups, lambda cp: cp.start())

    @pl.when(i == pl.num_programs(0) - 1)
    def _():
        zero_ref[...] = jnp.zeros_like(zero_ref)
        for act in (lambda cp: cp.start(), lambda cp: cp.wait()):
            for g in range(N_GROUPS):
                def piece(off, size, g=g, act=act):
                    act(pltpu.make_async_copy(
                        zero_ref.at[pl.ds(off, size)],
                        xg_ref.at[pl.ds(pl.multiple_of(toff_ref[g] + off, SEG_ALIGN), size)], sem))

                _for_each_piece(tlen_ref[g], piece)

        def zero_tile(k, carry):
            cp = pltpu.make_async_copy(zero_ref, xg_ref.at[pl.ds(pl.multiple_of(k * MOE_TM, MOE_TM), MOE_TM)], sem)
            cp.start()
            cp.wait()
            return carry

        lax.fori_loop(nv_ref[0], xg_ref.shape[0] // MOE_TM, zero_tile, 0)

    _segment_copies(*to_groups, lambda cp: cp.wait())


def _dispatch(h1, comb, tabs, *, n_ffn_tiles):
    rows = h1.shape[0]
    row = lambda i, *_: (i, 0)
    grid_spec = pltpu.PrefetchScalarGridSpec(
        num_scalar_prefetch=len(tabs),
        grid=(rows // MOE_TM,),
        in_specs=[pl.BlockSpec((MOE_TM, D_MODEL), row), pl.BlockSpec((MOE_TM, LANES), row)],
        out_specs=pl.BlockSpec(memory_space=pl.ANY),
        scratch_shapes=[pltpu.VMEM((SORT_ROWS, X_EXT), BF16), pltpu.VMEM((MOE_TM, X_EXT), BF16),
                        pltpu.SemaphoreType.DMA(())],
    )
    return pl.pallas_call(
        _dispatch_kernel,
        grid_spec=grid_spec,
        out_shape=jax.ShapeDtypeStruct((n_ffn_tiles * MOE_TM, X_EXT), BF16),
        compiler_params=pltpu.CompilerParams(dimension_semantics=("arbitrary",), vmem_limit_bytes=VMEM_LIMIT),
        name="moe_dispatch",
    )(*tabs, h1, comb)


def _group_ffn_kernel(tg_ref, nv_ref, x_ref, wg_ref, wu_ref, wd_ref, y_ref):
    k = pl.program_id(0)

    @pl.when(k < nv_ref[0])
    def _():
        g = tg_ref[k]
        xe = x_ref[...]
        hb = xe[:, :D_MODEL]
        comb = xe[:, D_MODEL:D_MODEL + LANES].astype(F32) + xe[:, D_MODEL + LANES:].astype(F32)
        hid = _silu(_dot(hb, wg_ref[0])) * _dot(hb, wu_ref[0])
        lane = lax.broadcasted_iota(jnp.int32, comb.shape, 1)
        parts = []
        for e in range(EXPERTS_PER_GROUP):
            col = ROUTER_COL0 + g * EXPERTS_PER_GROUP + e
            c = jnp.sum(jnp.where(lane == col, comb, 0.0), -1, keepdims=True)
            parts.append(hid[:, e * EXPERT_FF:(e + 1) * EXPERT_FF] * c)
        y_ref[...] = _dot(jnp.concatenate(parts, axis=-1).astype(BF16), wd_ref[0])

    @pl.when(k >= nv_ref[0])
    def _():
        y_ref[...] = jnp.zeros_like(y_ref)


def _group_ffn(xg, tile_group, n_valid, wg, wu, wd):
    n_tiles = xg.shape[0] // MOE_TM
    row = lambda k, tg, nv: (jnp.minimum(k, nv[0] - 1), 0)
    out_row = lambda k, tg, nv: (k, 0)
    grp = lambda k, tg, nv: (tg[k], 0, 0)
    grid_spec = pltpu.PrefetchScalarGridSpec(
        num_scalar_prefetch=2,
        grid=(n_tiles,),
        in_specs=[
            pl.BlockSpec((MOE_TM, X_EXT), row),
            pl.BlockSpec((1,) + wg.shape[1:], grp),
            pl.BlockSpec((1,) + wu.shape[1:], grp),
            pl.BlockSpec((1,) + wd.shape[1:], grp),
        ],
        out_specs=pl.BlockSpec((MOE_TM, D_MODEL), out_row),
    )
    return pl.pallas_call(
        _group_ffn_kernel,
        grid_spec=grid_spec,
        out_shape=jax.ShapeDtypeStruct((n_tiles * MOE_TM, D_MODEL), F32),
        compiler_params=pltpu.CompilerParams(dimension_semantics=("arbitrary",), vmem_limit_bytes=VMEM_LIMIT),
        name="moe_ffn",
    )(tile_group, n_valid, xg, wg, wu, wd)


def _combine_kernel(len_ref, soff_ref, doff_ref, h1_ref, comb_ref, g2_ref, b2_ref, ys_ref, o_ref, ysb_ref, sem):
    i = pl.program_id(0)
    n = pl.num_programs(0)

    def fetch(tile, slot, act):
        _segment_copies(tile, len_ref, ys_ref, doff_ref, ysb_ref.at[slot], soff_ref, sem.at[slot], act)

    def start_fetch(tile, slot):
        ysb_ref[slot] = jnp.zeros(ysb_ref.shape[1:], F32)
        fetch(tile, slot, lambda cp: cp.start())

    @pl.when(i == 0)
    def _():
        start_fetch(0, 0)

    for slot in range(2):
        @pl.when((i + 1 < n) & ((i + 1) % 2 == slot))
        def _(slot=slot):
            start_fetch(i + 1, slot)

    pt = _sorted_one_hot(comb_ref[...], soff_ref, i)
    for slot in range(2):
        @pl.when(i % 2 == slot)
        def _(slot=slot):
            fetch(i, slot, lambda cp: cp.wait())

    ys = ysb_ref[i % 2]
    hi = ys.astype(BF16)
    lo = (ys - hi.astype(F32)).astype(BF16)
    ffn = _dot(pt, hi) + _dot(pt, lo)
    o_ref[...] = _layer_norm(DEEPNORM_ALPHA * h1_ref[...] + ffn, g2_ref[...], b2_ref[...])


def _combine(h1, comb, ys, g2, b2, tabs):
    rows = h1.shape[0]
    row = lambda i, *_: (i, 0)
    const = lambda i, *_: (0, 0)
    grid_spec = pltpu.PrefetchScalarGridSpec(
        num_scalar_prefetch=len(tabs),
        grid=(rows // MOE_TM,),
        in_specs=[pl.BlockSpec((MOE_TM, D_MODEL), row), pl.BlockSpec((MOE_TM, LANES), row),
                  pl.BlockSpec((1, D_MODEL), const), pl.BlockSpec((1, D_MODEL), const),
                  pl.BlockSpec(memory_space=pl.ANY)],
        out_specs=pl.BlockSpec((MOE_TM, D_MODEL), row),
        scratch_shapes=[pltpu.VMEM((2, SORT_ROWS, D_MODEL), F32), pltpu.SemaphoreType.DMA((2,))],
    )
    return pl.pallas_call(
        _combine_kernel,
        grid_spec=grid_spec,
        out_shape=jax.ShapeDtypeStruct((rows, D_MODEL), F32),
        compiler_params=pltpu.CompilerParams(dimension_semantics=("arbitrary",), vmem_limit_bytes=VMEM_LIMIT),
        name="moe_combine",
    )(*tabs, h1, comb, g2, b2, ys)


def _moe_tables(cnt, rows):
    n_tiles = cnt.shape[0]
    up = lambda a, m: (a + m - 1) // m * m
    seg = up(cnt, SEG_ALIGN)
    soff = jnp.cumsum(seg, axis=1) - seg
    gtot = jnp.sum(seg, axis=0)
    gpad = up(gtot, MOE_TM)
    gbase = jnp.cumsum(gpad) - gpad
    doff = gbase[None, :] + jnp.cumsum(seg, axis=0) - seg
    n_ffn_tiles = -(-(rows + n_tiles * N_GROUPS * (SEG_ALIGN - 1)) // MOE_TM) + N_GROUPS
    starts = jnp.arange(n_ffn_tiles, dtype=jnp.int32) * MOE_TM
    tile_group = jnp.minimum(jnp.sum(starts[:, None] >= (gbase + gpad)[None, :], axis=1), N_GROUPS - 1)
    n_valid = (jnp.sum(gpad) // MOE_TM).reshape(1)
    i32 = lambda a: a.reshape(-1).astype(jnp.int32)
    return dict(seg=i32(seg), soff=i32(soff), doff=i32(doff), tlen=i32(gpad - gtot), toff=i32(gbase + gtot),
                tile_group=i32(tile_group), n_valid=i32(n_valid)), n_ffn_tiles


def _pad_lanes(a, col0=0):
    return jnp.pad(a, ((0, 0), (col0, LANES - col0 - a.shape[1])))


def kernel(x, meta_tokens, ln_emb_g, ln_emb_b, w_in, b_gate, dn_conv_w, dn_a_log, dn_dt_bias, dn_norm_g,
           w_branch_dn, w_branch_sb, w_out, ln1_g, ln1_b, router_group_w, router_group_b, router_expert_w,
           router_expert_b, expert_w_gate, expert_w_up, expert_w_down, ln2_g, ln2_b):
    batch, seq, d = x.shape
    assert d == D_MODEL and w_in.shape[0] == 1 and seq % max(GDN_BLOCK, SB_T) == 0 and batch % GDN_STREAMS == 0
    rows = batch * seq
    tm = 512
    assert rows % tm == 0
    row1 = lambda a: a.reshape(1, -1).astype(F32)

    w = w_in[0]
    c0 = 3 * DN_WIDTH
    c1 = c0 + DN_WIDTH
    c2 = c1 + 2 * DN_HEADS
    c3 = c2 + 3 * SB_WIDTH
    w_dn = w[:, :c0].astype(BF16)
    w_ba = _pad_lanes(w[:, c1:c2]).astype(BF16)
    w_sb = w[:, c2:c3].astype(BF16)
    w_zg = jnp.concatenate([w[:, c0:c1], w[:, c3:]], axis=1).astype(BF16)

    x2 = x.reshape(rows, d)
    g0, b0 = row1(ln_emb_g), row1(ln_emb_b)
    dn, sb, ba = _ln_proj(x2, g0, b0, w_dn, w_sb, w_ba, tm=tm)
    xm = jnp.concatenate([jnp.zeros((META_PAD, d), x.dtype), meta_tokens.astype(x.dtype)], axis=0)
    mdn, msb, mba = _ln_proj(xm, g0, b0, w_dn, w_sb, w_ba, tm=META_ROWS, n_zero=META_PAD)

    alog_row = _pad_lanes(dn_a_log[0].reshape(1, -1).astype(F32), DN_HEADS)
    dtb_row = _pad_lanes(dn_dt_bias[0].reshape(1, -1).astype(F32), DN_HEADS)
    o_dn = _gdn(dn, ba, mdn, mba, dn_conv_w[0].astype(F32), alog_row, dtb_row, row1(dn_norm_g[0]),
                batch=batch, seq=seq)
    o_sb = _sb_attn(sb, msb, batch=batch, seq=seq)

    w_r = _pad_lanes(jnp.concatenate(
        [router_group_w[0], router_expert_w[0].transpose(1, 0, 2).reshape(d, N_EXPERTS)], axis=1)).astype(BF16)
    b_r = _pad_lanes(jnp.concatenate(
        [router_group_b[0].reshape(1, -1), router_expert_b[0].reshape(1, -1)], axis=1).astype(F32))
    h1, comb, cnt = _mix_out(x2, o_dn, o_sb, g0, b0, w_zg, b_gate[0].astype(F32), w_branch_dn[0].astype(BF16),
                             w_branch_sb[0].astype(BF16), w_out[0].astype(BF16), row1(ln1_g[0]),
                             row1(ln1_b[0]), w_r, b_r, tm=MOE_TM)

    ef = EXPERTS_PER_GROUP * EXPERT_FF
    wg = expert_w_gate[0].transpose(0, 2, 1, 3).reshape(N_GROUPS, d, ef).astype(BF16)
    wu = expert_w_up[0].transpose(0, 2, 1, 3).reshape(N_GROUPS, d, ef).astype(BF16)
    wd = expert_w_down[0].reshape(N_GROUPS, ef, d).astype(BF16)
    tabs, n_ffn_tiles = _moe_tables(cnt[:, 0, :N_GROUPS].astype(jnp.int32), rows)
    seg_tabs = (tabs["seg"], tabs["soff"], tabs["doff"])
    xg = _dispatch(h1, comb, seg_tabs + (tabs["tlen"], tabs["toff"], tabs["n_valid"]), n_ffn_tiles=n_ffn_tiles)
    ys = _group_ffn(xg, tabs["tile_group"], tabs["n_valid"], wg, wu, wd)
    out = _combine(h1, comb, ys, row1(ln2_g[0]), row1(ln2_b[0]), seg_tabs)
    return out.reshape(batch, seq, d)
```

```python
import functools

import jax
import jax.numpy as jnp
import numpy as np
from jax import lax
from jax.experimental import pallas as pl
from jax.experimental.pallas import tpu as pltpu

F32 = jnp.float32
BF16 = jnp.bfloat16

D_MODEL = 1024
N_META = 16
DN_HEADS = 4
DN_HEAD_DIM = 128
DN_WIDTH = DN_HEADS * DN_HEAD_DIM
DN_CONV = 4
DN_CHUNK = 64
SB_HEADS = 8
SB_HEAD_DIM = 64
SB_WIDTH = SB_HEADS * SB_HEAD_DIM
N_GROUPS = 4
EXPERTS_PER_GROUP = 8
N_EXPERTS = N_GROUPS * EXPERTS_PER_GROUP
EXPERT_FF = 256
DEEPNORM_ALPHA = 2.0 ** 0.25
LN_EPS = 1e-5
RMS_EPS = 1e-6

LANES = 128
SUBLANES = 8
META_ROWS = DN_CHUNK
META_PAD = META_ROWS - N_META
ROUTER_COL0 = N_GROUPS
VMEM_LIMIT = 56 * 1024 * 1024


def _layer_norm(x, g, b):
    mu = jnp.mean(x, -1, keepdims=True)
    xc = x - mu
    var = jnp.mean(xc * xc, -1, keepdims=True)
    return xc * lax.rsqrt(var + LN_EPS) * g + b


def _sigmoid(x):
    return 1.0 / (1.0 + jnp.exp(-x))


def _softplus(x):
    return jnp.maximum(x, 0.0) + jnp.log(1.0 + jnp.exp(-jnp.abs(x)))


def _silu(x):
    return x * _sigmoid(x)


def _dot(a, b):
    return jnp.dot(a, b, preferred_element_type=F32)


def _dot_nt(a, b):
    return lax.dot_general(a, b, (((1,), (1,)), ((), ())), preferred_element_type=F32)


def _dot_tn(a, b):
    return lax.dot_general(a, b, (((0,), (0,)), ((), ())), preferred_element_type=F32)


def _ln_proj_kernel(x_ref, g_ref, b_ref, wdn_ref, wsb_ref, wba_ref, dn_ref, sb_ref, ba_ref, *, n_zero):
    h = _layer_norm(x_ref[...], g_ref[...], b_ref[...])
    if n_zero:
        rows = lax.broadcasted_iota(jnp.int32, h.shape, 0)
        h = jnp.where(rows >= n_zero, h, 0.0)
    hb = h.astype(BF16)
    dn_ref[...] = _dot(hb, wdn_ref[...])
    sb_ref[...] = _dot(hb, wsb_ref[...]).astype(BF16)
    ba_ref[...] = _dot(hb, wba_ref[...])


def _ln_proj(x2, g, b, wdn, wsb, wba, *, tm, n_zero=0):
    rows = x2.shape[0]
    const = lambda i: (0, 0)
    row = lambda i: (i, 0)
    return pl.pallas_call(
        functools.partial(_ln_proj_kernel, n_zero=n_zero),
        grid=(rows // tm,),
        in_specs=[
            pl.BlockSpec((tm, D_MODEL), row),
            pl.BlockSpec((1, D_MODEL), const),
            pl.BlockSpec((1, D_MODEL), const),
            pl.BlockSpec(wdn.shape, const),
            pl.BlockSpec(wsb.shape, const),
            pl.BlockSpec(wba.shape, const),
        ],
        out_specs=[
            pl.BlockSpec((tm, 3 * DN_WIDTH), row),
            pl.BlockSpec((tm, 3 * SB_WIDTH), row),
            pl.BlockSpec((tm, LANES), row),
        ],
        out_shape=[
            jax.ShapeDtypeStruct((rows, 3 * DN_WIDTH), F32),
            jax.ShapeDtypeStruct((rows, 3 * SB_WIDTH), BF16),
            jax.ShapeDtypeStruct((rows, LANES), F32),
        ],
        compiler_params=pltpu.CompilerParams(dimension_semantics=("arbitrary",), vmem_limit_bytes=VMEM_LIMIT),
        name="ln_proj",
    )(x2, g, b, wdn, wsb, wba)


GDN_BLOCK = 256
GDN_STREAMS = 1
GDN_CHUNK = DN_CHUNK
CONV_HIST = 8


def _gdn_rows(streams, n, n_zero, cw_ref, alog_ref, dtb_ref, ng_ref):
    c = min(n, GDN_CHUNK)
    ri = lax.broadcasted_iota(jnp.int32, (n, n), 0)
    ci = lax.broadcasted_iota(jnp.int32, (n, n), 1)
    same = (ri // c) == (ci // c)
    causal = same & (ri >= ci)
    strict = same & (ri > ci)
    tril = causal.astype(BF16)

    pre = []
    for src_ref, ba_ref, _, xbuf, _, vn_ref in streams:
        xbuf[CONV_HIST:CONV_HIST + n, :] = src_ref[...]
        acc = xbuf[CONV_HIST:CONV_HIST + n, :] * cw_ref[DN_CONV - 1:DN_CONV, :]
        for i in range(DN_CONV - 1):
            s = DN_CONV - 1 - i
            acc = acc + xbuf[CONV_HIST - s:CONV_HIST - s + n, :] * cw_ref[i:i + 1, :]
        hist = xbuf[n:n + CONV_HIST, :]
        xbuf[0:CONV_HIST, :] = hist
        qkv = _silu(acc)

        ba = ba_ref[...]
        beta_all = _sigmoid(ba)
        g_all = -jnp.exp(alog_ref[...]) * _softplus(ba + dtb_ref[...])
        if n_zero:
            rows = lax.broadcasted_iota(jnp.int32, g_all.shape, 0)
            g_all = jnp.where(rows >= n_zero, g_all, 0.0)
        g_hi = g_all.astype(BF16)
        g_lo = (g_all - g_hi.astype(F32)).astype(BF16)
        dec = _dot(tril, g_hi) + _dot(tril, g_lo)
        vn_ref[...] = jnp.zeros_like(vn_ref)
        pre.append((qkv, beta_all, dec, dec.T))

    units = [(si, h) for si in range(len(streams)) for h in range(DN_HEADS)]
    qe, ks, d_cols, qks, rhss, ps = [], [], [], [], [], []
    for si, h in units:
        qkv, beta_all, dec, dec_t = pre[si]
        q = qkv[:, h * DN_HEAD_DIM:(h + 1) * DN_HEAD_DIM]
        k = qkv[:, DN_WIDTH + h * DN_HEAD_DIM:DN_WIDTH + (h + 1) * DN_HEAD_DIM]
        v = qkv[:, 2 * DN_WIDTH + h * DN_HEAD_DIM:2 * DN_WIDTH + (h + 1) * DN_HEAD_DIM]
        q = q * lax.rsqrt(jnp.sum(q * q, -1, keepdims=True) + RMS_EPS) * (DN_HEAD_DIM ** -0.5)
        k = k * lax.rsqrt(jnp.sum(k * k, -1, keepdims=True) + RMS_EPS)
        beta = beta_all[:, h:h + 1]
        d_col = dec[:, DN_HEADS + h:DN_HEADS + h + 1]
        d_row = dec_t[DN_HEADS + h:DN_HEADS + h + 1, :]
        lmask = jnp.where(causal, jnp.exp(jnp.where(causal, d_col - d_row, 0.0)), 0.0)
        kb = k * beta
        k16 = k.astype(BF16)
        a = jnp.where(strict, _dot_nt(kb.astype(BF16), k16) * lmask, 0.0)
        qks.append((_dot_nt(q.astype(BF16), k16) * lmask).astype(BF16))
        e_col = jnp.exp(d_col)
        rhss.append(jnp.concatenate([v * beta, kb * e_col], axis=1))
        qe.append(q * e_col)
        ks.append(k)
        d_cols.append(d_col)
        ps.append(-a)
    ts = list(ps)
    for _ in range(c.bit_length() - 2):
        p16s = [p.astype(BF16) for p in ps]
        ps = [_dot(p16, p16) for p16 in p16s]
        ts = [t + p + _dot(t.astype(BF16), p.astype(BF16)) for t, p in zip(ts, ps)]
    uws = [rhs + _dot(t.astype(BF16), rhs.astype(BF16)) for t, rhs in zip(ts, rhss)]

    for ch in range(n // c):
        r0 = ch * c
        for u, (si, h) in enumerate(units):
            _, _, o_ref, _, s_ref, vn_ref = streams[si]
            k, d_col, qk, uw = ks[u], d_cols[u], qks[u], uws[u]
            d_c = d_col[r0:r0 + c, :]
            d_last = d_col[r0 + c - 1:r0 + c, :]
            s = s_ref[h]
            wq = jnp.concatenate([uw[r0:r0 + c, DN_HEAD_DIM:], qe[u][r0:r0 + c, :]], axis=0)
            r = _dot(wq.astype(BF16), s.astype(BF16))
            v_new = uw[r0:r0 + c, :DN_HEAD_DIM] - r[:c, :]
            vn16 = v_new.astype(BF16)
            vn_ref[h, r0:r0 + c, :] = vn16
            o = r[c:, :] + _dot(qk[r0:r0 + c, :], vn_ref[h])
            k_dec = k[r0:r0 + c, :] * jnp.exp(d_last - d_c)
            s_ref[h] = s * jnp.exp(d_last) + _dot_tn(k_dec.astype(BF16), vn16)
            if o_ref is not None:
                o_n = o * lax.rsqrt(jnp.mean(o * o, -1, keepdims=True) + RMS_EPS) * ng_ref[...]
                o_ref[r0:r0 + c, h * DN_HEAD_DIM:(h + 1) * DN_HEAD_DIM] = o_n


def _gdn_kernel(dn_ref, ba_ref, mdn_ref, mba_ref, cw_ref, alog_ref, dtb_ref, ng_ref, o_ref, xbuf, s_ref, vn_ref,
                vnm_ref):
    @pl.when(pl.program_id(1) == 0)
    def _():
        s_ref[0] = jnp.zeros(s_ref.shape[1:], F32)
        xbuf[0, 0:CONV_HIST, :] = jnp.zeros((CONV_HIST, xbuf.shape[2]), F32)
        _gdn_rows([(mdn_ref, mba_ref, None, xbuf.at[0], s_ref.at[0], vnm_ref)], META_ROWS, META_PAD,
                  cw_ref, alog_ref, dtb_ref, ng_ref)
        for si in range(1, GDN_STREAMS):
            s_ref[si] = s_ref[0]
            xbuf[si, 0:CONV_HIST, :] = xbuf[0, 0:CONV_HIST, :]

    _gdn_rows([(dn_ref.at[si], ba_ref.at[si], o_ref.at[si], xbuf.at[si], s_ref.at[si], vn_ref.at[si])
               for si in range(GDN_STREAMS)], GDN_BLOCK, 0, cw_ref, alog_ref, dtb_ref, ng_ref)


def _gdn(dn, ba, mdn, mba, conv_w, alog_row, dtb_row, norm_g, *, batch, seq):
    nb = seq // GDN_BLOCK
    const = lambda b, j: (0, 0)
    row = lambda b, j: (b, j, 0)
    dn = dn.reshape(batch, seq, dn.shape[-1])
    ba = ba.reshape(batch, seq, ba.shape[-1])
    out = pl.pallas_call(
        _gdn_kernel,
        grid=(batch // GDN_STREAMS, nb),
        in_specs=[
            pl.BlockSpec((GDN_STREAMS, GDN_BLOCK, 3 * DN_WIDTH), row),
            pl.BlockSpec((GDN_STREAMS, GDN_BLOCK, LANES), row),
            pl.BlockSpec((META_ROWS, 3 * DN_WIDTH), const),
            pl.BlockSpec((META_ROWS, LANES), const),
            pl.BlockSpec((DN_CONV, 3 * DN_WIDTH), const),
            pl.BlockSpec((1, LANES), const),
            pl.BlockSpec((1, LANES), const),
            pl.BlockSpec((1, DN_HEAD_DIM), const),
        ],
        out_specs=pl.BlockSpec((GDN_STREAMS, GDN_BLOCK, DN_WIDTH), row),
        out_shape=jax.ShapeDtypeStruct((batch, seq, DN_WIDTH), F32),
        scratch_shapes=[
            pltpu.VMEM((GDN_STREAMS, GDN_BLOCK + CONV_HIST, 3 * DN_WIDTH), F32),
            pltpu.VMEM((GDN_STREAMS, DN_HEADS, DN_HEAD_DIM, DN_HEAD_DIM), F32),
            pltpu.VMEM((GDN_STREAMS, DN_HEADS, GDN_BLOCK, DN_HEAD_DIM), BF16),
            pltpu.VMEM((DN_HEADS, META_ROWS, DN_HEAD_DIM), BF16),
        ],
        compiler_params=pltpu.CompilerParams(dimension_semantics=("arbitrary", "arbitrary"),
                                             vmem_limit_bytes=VMEM_LIMIT),
        name="gdn",
    )(dn, ba, mdn, mba, conv_w, alog_row, dtb_row, norm_g)
    return out.reshape(batch * seq, DN_WIDTH)


SB_T = 2 * LANES
HEADS_PER_BLOCK = LANES // SB_HEAD_DIM
SB_SLOTS = 4
SB_UNROLL = 16
NEG_BIG = -1e30


def _sb_schedule(nq):
    qoff, koff, bsel, first = [], [], [], []
    for qi in range(nq):
        for kj in range(qi, -1, -1):
            qoff.append(qi * SB_T)
            koff.append(kj * SB_T)
            bsel.append(1 if kj == qi else 0)
            first.append(1 if kj == qi else 0)
    return [np.asarray(a, np.int32) for a in (qoff, koff, bsel, first)]


def _sb_kernel(qoff_ref, koff_ref, bsel_ref, first_ref, q_ref, k_ref, v_ref, mk_ref, mv_ref, o_ref,
               qm_s, bias_s, nu_s, z_s, sp_s, later_s, w_s, acc_s, carry_s, accq_s, carryq_s, *, n_tiles):
    t = SB_T
    ri = lax.broadcasted_iota(jnp.int32, (t, t), 0)
    ci = lax.broadcasted_iota(jnp.int32, (t, t), 1)
    bias_s[0] = jnp.zeros((t, t), F32)
    bias_s[1] = jnp.where(ci < ri, 0.0, NEG_BIG)
    acc_s[...] = jnp.zeros_like(acc_s)
    carry_s[...] = jnp.zeros_like(carry_s)
    nu_s[...] = jnp.where(ri > ci, -1.0, 0.0).astype(BF16)
    lane = lax.broadcasted_iota(jnp.int32, (t, LANES), 1)
    scale = SB_HEAD_DIM ** -0.5

    q_all = q_ref[...]
    lane_q = lax.broadcasted_iota(jnp.int32, q_all.shape, 1)
    for h in range(HEADS_PER_BLOCK):
        qm_s[h] = jnp.where((lane_q // SB_HEAD_DIM) == h, q_all, jnp.zeros_like(q_all)) * scale

    def st_logits(n, s):
        qo = pl.multiple_of(qoff_ref[n], t)
        ko = pl.multiple_of(koff_ref[n], t)
        kt = k_ref[pl.ds(ko, t), :]
        bias = bias_s[bsel_ref[n]]
        for h in range(HEADS_PER_BLOCK):
            z_s[s, h] = _dot_nt(qm_s[h, pl.ds(qo, t), :], kt) + bias

    def st_softplus(n, s):
        for h in range(HEADS_PER_BLOCK):
            z = z_s[s, h]
            sp = jnp.maximum(z, 0.0) + jnp.log(1.0 + jnp.exp(-jnp.abs(z)))
            sp_s[s, h] = sp.astype(BF16)
            z_s[s, h] = z - sp

    def st_keysum(n, s):
        for h in range(HEADS_PER_BLOCK):
            later_s[s, h] = _dot(sp_s[s, h], nu_s[...])

    def st_weights(n, s):
        keep = jnp.where(first_ref[n] == 1, 0.0, 1.0)
        qb = qoff_ref[n] // t
        for h in range(HEADS_PER_BLOCK):
            later = later_s[s, h]
            carry = carry_s[h] * keep
            logw = z_s[s, h] + later + jnp.concatenate([carry] * (t // LANES), axis=1)
            w_s[s, h] = jnp.exp(logw).astype(BF16)
            total = later[:, 0:1] - sp_s[s, h, :, 0:1].astype(F32)
            carry = carry + jnp.broadcast_to(total, (t, LANES))
            carry_s[h] = carry
            carryq_s[qb, h] = carry

    def st_values(n, s):
        ko = pl.multiple_of(koff_ref[n], t)
        vt = v_ref[pl.ds(ko, t), :]
        keep = jnp.where(first_ref[n] == 1, 0.0, 1.0)
        qb = qoff_ref[n] // t
        for h in range(HEADS_PER_BLOCK):
            acc = acc_s[h] * keep + _dot(w_s[s, h], vt)
            acc_s[h] = acc
            accq_s[qb, h] = acc

    stages = (st_logits, st_softplus, st_keysum, st_weights, st_values)
    depth = len(stages)

    def trip(it, parity):
        for d in reversed(range(depth)):
            n = it - d
            if isinstance(n, int) and not 0 <= n < n_tiles:
                continue
            stages[d](n, (parity - d) % SB_SLOTS)

    for it in range(depth - 1):
        trip(it, it % SB_SLOTS)
    first_full = depth - 1
    n_full = n_tiles - first_full
    n_loop = n_full // SB_UNROLL

    def body(u, c):
        base = first_full + u * SB_UNROLL
        for j in range(SB_UNROLL):
            trip(base + j, (first_full + j) % SB_SLOTS)
        return c

    lax.fori_loop(0, n_loop, body, 0)
    for it in range(first_full + n_loop * SB_UNROLL, n_tiles + depth - 1):
        trip(it, it % SB_SLOTS)

    m = META_ROWS
    mrow = lax.broadcasted_iota(jnp.int32, (HEADS_PER_BLOCK * m, LANES), 0)
    mlane = lax.broadcasted_iota(jnp.int32, (HEADS_PER_BLOCK * m, LANES), 1)
    own = (mrow // m) == (mlane // SB_HEAD_DIM)
    mkk = jnp.where(own, jnp.concatenate([mk_ref[...]] * HEADS_PER_BLOCK, axis=0), jnp.zeros((), BF16))
    mvv = jnp.where(own, jnp.concatenate([mv_ref[...]] * HEADS_PER_BLOCK, axis=0), jnp.zeros((), BF16))
    mr = lax.broadcasted_iota(jnp.int32, (LANES, LANES), 0)
    mc = lax.broadcasted_iota(jnp.int32, (LANES, LANES), 1)
    nu_m = jnp.where((mr > mc) & ((mr // m) == (mc // m)), -1.0, 0.0).astype(BF16)
    mbias = jnp.where((lane % m) >= META_PAD, 0.0, NEG_BIG)
    head1 = (lane // SB_HEAD_DIM) == 1
    for qb in range(q_ref.shape[0] // t):
        rows = slice(qb * t, (qb + 1) * t)
        z = _dot_nt(q_ref[rows, :] * scale, mkk) + mbias
        sp = jnp.maximum(z, 0.0) + jnp.log(1.0 + jnp.exp(-jnp.abs(z)))
        later = _dot(sp.astype(BF16), nu_m)
        carry = jnp.where(head1, carryq_s[qb, 1], carryq_s[qb, 0])
        w = jnp.exp(z - sp + later + carry)
        acc = jnp.where(head1, accq_s[qb, 1], accq_s[qb, 0]) + _dot(w.astype(BF16), mvv)
        o_ref[rows, :] = acc.astype(o_ref.dtype)


def _sb_attn(sb, msb, *, batch, seq):
    nq = seq // SB_T
    n_hb = SB_WIDTH // LANES
    sched = _sb_schedule(nq)
    n_tiles = int(sched[0].shape[0])
    grid_spec = pltpu.PrefetchScalarGridSpec(
        num_scalar_prefetch=len(sched),
        grid=(batch, n_hb),
        in_specs=[
            pl.BlockSpec((seq, LANES), lambda b, hp, *_: (b, hp)),
            pl.BlockSpec((seq, LANES), lambda b, hp, *_: (b, n_hb + hp)),
            pl.BlockSpec((seq, LANES), lambda b, hp, *_: (b, 2 * n_hb + hp)),
            pl.BlockSpec((META_ROWS, LANES), lambda b, hp, *_: (0, n_hb + hp)),
            pl.BlockSpec((META_ROWS, LANES), lambda b, hp, *_: (0, 2 * n_hb + hp)),
        ],
        out_specs=pl.BlockSpec((seq, LANES), lambda b, hp, *_: (b, hp)),
        scratch_shapes=[
            pltpu.VMEM((HEADS_PER_BLOCK, seq, LANES), BF16),
            pltpu.VMEM((2, SB_T, SB_T), F32),
            pltpu.VMEM((SB_T, SB_T), BF16),
            pltpu.VMEM((SB_SLOTS, HEADS_PER_BLOCK, SB_T, SB_T), F32),
            pltpu.VMEM((SB_SLOTS, HEADS_PER_BLOCK, SB_T, SB_T), BF16),
            pltpu.VMEM((SB_SLOTS, HEADS_PER_BLOCK, SB_T, SB_T), F32),
            pltpu.VMEM((SB_SLOTS, HEADS_PER_BLOCK, SB_T, SB_T), BF16),
            pltpu.VMEM((HEADS_PER_BLOCK, SB_T, LANES), F32),
            pltpu.VMEM((HEADS_PER_BLOCK, SB_T, LANES), F32),
            pltpu.VMEM((nq, HEADS_PER_BLOCK, SB_T, LANES), F32),
            pltpu.VMEM((nq, HEADS_PER_BLOCK, SB_T, LANES), F32),
        ],
    )
    return pl.pallas_call(
        functools.partial(_sb_kernel, n_tiles=n_tiles),
        grid_spec=grid_spec,
        out_shape=jax.ShapeDtypeStruct((batch * seq, SB_WIDTH), BF16),
        compiler_params=pltpu.CompilerParams(dimension_semantics=("arbitrary", "arbitrary"),
                                             vmem_limit_bytes=VMEM_LIMIT),
        name="sb_attn",
    )(*[jnp.asarray(a) for a in sched], sb, sb, sb, msb, msb)


MIX_CHAINS = 4


def _masked_lane_max(x, mask):
    return jnp.max(jnp.where(mask, x, -jnp.inf), -1, keepdims=True)


def _first_lane_eq(x, val, mask, lane):
    return jnp.min(jnp.where(mask & (x == val), lane, LANES), -1, keepdims=True)


def _route(logits):
    lane = lax.broadcasted_iota(jnp.int32, logits.shape, 1)
    gmask = lane < N_GROUPS
    gmax = _masked_lane_max(logits, gmask)
    g_idx = _first_lane_eq(logits, gmax, gmask, lane)
    g_prob = 1.0 / jnp.sum(jnp.where(gmask, jnp.exp(logits - gmax), 0.0), -1, keepdims=True)
    lo = ROUTER_COL0 + g_idx * EXPERTS_PER_GROUP
    emask = (lane >= lo) & (lane < lo + EXPERTS_PER_GROUP)
    t1 = _masked_lane_max(logits, emask)
    i1 = _first_lane_eq(logits, t1, emask, lane)
    emask2 = emask & (lane != i1)
    t2 = _masked_lane_max(logits, emask2)
    i2 = _first_lane_eq(logits, t2, emask2, lane)
    e = jnp.exp(t2 - t1)
    w1 = g_prob / (1.0 + e)
    w2 = g_prob * e / (1.0 + e)
    comb = jnp.where(lane == i1, w1, 0.0) + jnp.where(lane == i2, w2, 0.0)
    comb = jnp.where(lane == 0, g_idx.astype(F32), comb)
    counts = jnp.sum((lane == g_idx).astype(F32), 0, keepdims=True)
    return comb, counts


def _mix_out_kernel(x_ref, odn_ref, osb_ref, g0_ref, b0_ref, wzg_ref, bg_ref, wbdn_ref, wbsb_ref, wout_ref,
                    g1_ref, b1_ref, wr_ref, br_ref, h1_ref, comb_ref, cnt_ref):
    sub = x_ref.shape[0] // MIX_CHAINS
    rows = [slice(r0, r0 + sub) for r0 in range(0, x_ref.shape[0], sub)]
    h0s = [_layer_norm(x_ref[rs, :], g0_ref[...], b0_ref[...]) for rs in rows]
    zgs = [_dot(h0.astype(BF16), wzg_ref[...]) for h0 in h0s]
    bs = [_dot(osb_ref[rs, :], wbsb_ref[...]) for rs in rows]
    o_dns = [odn_ref[rs, :] * _silu(zg[:, :DN_WIDTH]) for rs, zg in zip(rows, zgs)]
    as_ = [_dot(o_dn.astype(BF16), wbdn_ref[...]) for o_dn in o_dns]
    merged = [_sigmoid(zg[:, DN_WIDTH:DN_WIDTH + D_MODEL] + bg_ref[0:1, :]) * a
              + _sigmoid(zg[:, DN_WIDTH + D_MODEL:] + bg_ref[1:2, :]) * b for zg, a, b in zip(zgs, as_, bs)]
    mixes = [_dot(m.astype(BF16), wout_ref[...]) for m in merged]
    h1s = [_layer_norm(DEEPNORM_ALPHA * h0 + mix, g1_ref[...], b1_ref[...]) for h0, mix in zip(h0s, mixes)]
    logits = [_dot(h1.astype(BF16), wr_ref[...]) + br_ref[...] for h1 in h1s]
    counts = jnp.zeros((1, LANES), F32)
    for rs, h1, lg in zip(rows, h1s, logits):
        h1_ref[rs, :] = h1
        comb, cnt = _route(lg)
        comb_ref[rs, :] = comb
        counts = counts + cnt
    cnt_ref[0] = jnp.broadcast_to(counts, cnt_ref.shape[1:])


def _mix_out(x2, o_dn, o_sb, g0, b0, wzg, bg, wbdn, wbsb, wout, g1, b1, wr, br, *, tm):
    rows = x2.shape[0]
    const = lambda i: (0, 0)
    row = lambda i: (i, 0)
    full = lambda a: pl.BlockSpec(a.shape, const)
    return pl.pallas_call(
        _mix_out_kernel,
        grid=(rows // tm,),
        in_specs=[
            pl.BlockSpec((tm, D_MODEL), row),
            pl.BlockSpec((tm, DN_WIDTH), row),
            pl.BlockSpec((tm, SB_WIDTH), row),
            full(g0), full(b0), full(wzg), full(bg), full(wbdn), full(wbsb), full(wout), full(g1), full(b1),
            full(wr), full(br),
        ],
        out_specs=[pl.BlockSpec((tm, D_MODEL), row), pl.BlockSpec((tm, LANES), row),
                   pl.BlockSpec((1, SUBLANES, LANES), lambda i: (i, 0, 0))],
        out_shape=[jax.ShapeDtypeStruct((rows, D_MODEL), F32), jax.ShapeDtypeStruct((rows, LANES), F32),
                   jax.ShapeDtypeStruct((rows // tm, SUBLANES, LANES), F32)],
        compiler_params=pltpu.CompilerParams(dimension_semantics=("arbitrary",), vmem_limit_bytes=VMEM_LIMIT),
        name="mix_out",
    )(x2, o_dn, o_sb, g0, b0, wzg, bg, wbdn, wbsb, wout, g1, b1, wr, br)


MOE_TM = 512
SEG_ALIGN = 16
SEG_SIZES = (512, 256, 128, 64, 32, 16)
SORT_ROWS = MOE_TM + 64
X_EXT = D_MODEL + 2 * LANES


def _sorted_one_hot(comb, soff_ref, i):
    tm = comb.shape[0]
    lane = lax.broadcasted_iota(jnp.int32, (tm, LANES), 1)
    onehot = lane == comb[:, 0:1].astype(jnp.int32)
    ri = lax.broadcasted_iota(jnp.int32, (tm, tm), 0)
    ci = lax.broadcasted_iota(jnp.int32, (tm, tm), 1)
    ranks = _dot((ri > ci).astype(BF16), onehot.astype(BF16))
    start = jnp.zeros((1, LANES), F32)
    for g in range(N_GROUPS):
        start = jnp.where(lane[0:1, :] == g, soff_ref[i * N_GROUPS + g].astype(F32), start)
    pos = jnp.sum(jnp.where(onehot, ranks + start, 0.0), -1, keepdims=True)
    slot = lax.broadcasted_iota(jnp.int32, (tm, SORT_ROWS), 1)
    return (slot == pos.astype(jnp.int32)).astype(BF16)


def _for_each_piece(length, fn):
    for size in SEG_SIZES:
        @pl.when((length & size) != 0)
        def _(size=size):
            fn(pl.multiple_of(length & (-2 * size), SEG_ALIGN), size)


def _segment_copies(i, len_ref, src_ref, soff_ref, dst_ref, doff_ref, sem, act):
    for g in range(N_GROUPS):
        n = i * N_GROUPS + g
        so = soff_ref[n]
        do = doff_ref[n]

        def piece(off, size, so=so, do=do):
            act(pltpu.make_async_copy(src_ref.at[pl.ds(pl.multiple_of(so + off, SEG_ALIGN), size)],
                                      dst_ref.at[pl.ds(pl.multiple_of(do + off, SEG_ALIGN), size)], sem))

        _for_each_piece(len_ref[n], piece)


def _dispatch_kernel(len_ref, soff_ref, doff_ref, tlen_ref, toff_ref, nv_ref, h1_ref, comb_ref, xg_ref, xs_ref,
                     zero_ref, sem):
    i = pl.program_id(0)
    comb = comb_ref[...]
    pt = _sorted_one_hot(comb, soff_ref, i)
    lane = lax.broadcasted_iota(jnp.int32, comb.shape, 1)
    c = jnp.where(lane >= ROUTER_COL0, comb, 0.0)
    c_hi = c.astype(BF16)
    c_lo = (c - c_hi.astype(F32)).astype(BF16)
    src = jnp.concatenate([h1_ref[...].astype(BF16), c_hi, c_lo], axis=1)
    xs_ref[...] = _dot_tn(pt, src).astype(BF16)
    to_groups = (i, len_ref, xs_ref, soff_ref, xg_ref, doff_ref, sem)
    _segment_copies(*to_groups, lambda cp: cp.start())

    @pl.when(i == pl.num_programs(0) - 1)
    def _():
        zero_ref[...] = jnp.zeros_like(zero_ref)
        for act in (lambda cp: cp.start(), lambda cp: cp.wait()):
            for g in range(N_GROUPS):
                def piece(off, size, g=g, act=act):
                    act(pltpu.make_async_copy(
                        zero_ref.at[pl.ds(off, size)],
                        xg_ref.at[pl.ds(pl.multiple_of(toff_ref[g] + off, SEG_ALIGN), size)], sem))

                _for_each_piece(tlen_ref[g], piece)

        def zero_tile(k, carry):
            cp = pltpu.make_async_copy(zero_ref, xg_ref.at[pl.ds(pl.multiple_of(k * MOE_TM, MOE_TM), MOE_TM)], sem)
            cp.start()
            cp.wait()
            return carry

        lax.fori_loop(nv_ref[0], xg_ref.shape[0] // MOE_TM, zero_tile, 0)

    _segment_copies(*to_groups, lambda cp: cp.wait())


def _dispatch(h1, comb, tabs, *, n_ffn_tiles):
    rows = h1.shape[0]
    row = lambda i, *_: (i, 0)
    grid_spec = pltpu.PrefetchScalarGridSpec(
        num_scalar_prefetch=len(tabs),
        grid=(rows // MOE_TM,),
        in_specs=[pl.BlockSpec((MOE_TM, D_MODEL), row), pl.BlockSpec((MOE_TM, LANES), row)],
        out_specs=pl.BlockSpec(memory_space=pl.ANY),
        scratch_shapes=[pltpu.VMEM((SORT_ROWS, X_EXT), BF16), pltpu.VMEM((MOE_TM, X_EXT), BF16),
                        pltpu.SemaphoreType.DMA(())],
    )
    return pl.pallas_call(
        _dispatch_kernel,
        grid_spec=grid_spec,
        out_shape=jax.ShapeDtypeStruct((n_ffn_tiles * MOE_TM, X_EXT), BF16),
        compiler_params=pltpu.CompilerParams(dimension_semantics=("arbitrary",), vmem_limit_bytes=VMEM_LIMIT),
        name="moe_dispatch",
    )(*tabs, h1, comb)


def _group_ffn_kernel(tg_ref, nv_ref, x_ref, wg_ref, wu_ref, wd_ref, y_ref):
    k = pl.program_id(0)

    @pl.when(k < nv_ref[0])
    def _():
        g = tg_ref[k]
        xe = x_ref[...]
        hb = xe[:, :D_MODEL]
        comb = xe[:, D_MODEL:D_MODEL + LANES].astype(F32) + xe[:, D_MODEL + LANES:].astype(F32)
        lane = lax.broadcasted_iota(jnp.int32, comb.shape, 1)
        parts = []
        for e in range(EXPERTS_PER_GROUP):
            col = ROUTER_COL0 + g * EXPERTS_PER_GROUP + e
            c = jnp.sum(jnp.where(lane == col, comb, 0.0), -1, keepdims=True)
            parts.append(_silu(_dot(hb, wg_ref[0, e])) * _dot(hb, wu_ref[0, e]) * c)
        y_ref[...] = _dot(jnp.concatenate(parts, axis=-1).astype(BF16), wd_ref[0])

    @pl.when(k >= nv_ref[0])
    def _():
        y_ref[...] = jnp.zeros_like(y_ref)


def _group_ffn(xg, tile_group, n_valid, wg, wu, wd):
    n_tiles = xg.shape[0] // MOE_TM
    row = lambda k, tg, nv: (jnp.minimum(k, nv[0] - 1), 0)
    out_row = lambda k, tg, nv: (k, 0)
    grp = lambda k, tg, nv: (tg[k], 0, 0)
    grid_spec = pltpu.PrefetchScalarGridSpec(
        num_scalar_prefetch=2,
        grid=(n_tiles,),
        in_specs=[
            pl.BlockSpec((MOE_TM, X_EXT), row),
            pl.BlockSpec((1,) + wg.shape[1:], lambda k, tg, nv: (tg[k], 0, 0, 0)),
            pl.BlockSpec((1,) + wu.shape[1:], lambda k, tg, nv: (tg[k], 0, 0, 0)),
            pl.BlockSpec((1,) + wd.shape[1:], grp),
        ],
        out_specs=pl.BlockSpec((MOE_TM, D_MODEL), out_row),
    )
    return pl.pallas_call(
        _group_ffn_kernel,
        grid_spec=grid_spec,
        out_shape=jax.ShapeDtypeStruct((n_tiles * MOE_TM, D_MODEL), F32),
        compiler_params=pltpu.CompilerParams(dimension_semantics=("arbitrary",), vmem_limit_bytes=VMEM_LIMIT),
        name="moe_ffn",
    )(tile_group, n_valid, xg, wg, wu, wd)


def _combine_kernel(len_ref, soff_ref, doff_ref, h1_ref, comb_ref, g2_ref, b2_ref, ys_ref, o_ref, ysb_ref, sem):
    i = pl.program_id(0)
    n = pl.num_programs(0)

    def fetch(tile, slot, act):
        _segment_copies(tile, len_ref, ys_ref, doff_ref, ysb_ref.at[slot], soff_ref, sem.at[slot], act)

    def start_fetch(tile, slot):
        ysb_ref[slot] = jnp.zeros(ysb_ref.shape[1:], F32)
        fetch(tile, slot, lambda cp: cp.start())

    @pl.when(i == 0)
    def _():
        start_fetch(0, 0)

    for slot in range(2):
        @pl.when((i + 1 < n) & ((i + 1) % 2 == slot))
        def _(slot=slot):
            start_fetch(i + 1, slot)

    pt = _sorted_one_hot(comb_ref[...], soff_ref, i)
    for slot in range(2):
        @pl.when(i % 2 == slot)
        def _(slot=slot):
            fetch(i, slot, lambda cp: cp.wait())

    ys = ysb_ref[i % 2]
    hi = ys.astype(BF16)
    lo = (ys - hi.astype(F32)).astype(BF16)
    ffn = _dot(pt, hi) + _dot(pt, lo)
    o_ref[...] = _layer_norm(DEEPNORM_ALPHA * h1_ref[...] + ffn, g2_ref[...], b2_ref[...])


def _combine(h1, comb, ys, g2, b2, tabs):
    rows = h1.shape[0]
    row = lambda i, *_: (i, 0)
    const = lambda i, *_: (0, 0)
    grid_spec = pltpu.PrefetchScalarGridSpec(
        num_scalar_prefetch=len(tabs),
        grid=(rows // MOE_TM,),
        in_specs=[pl.BlockSpec((MOE_TM, D_MODEL), row), pl.BlockSpec((MOE_TM, LANES), row),
                  pl.BlockSpec((1, D_MODEL), const), pl.BlockSpec((1, D_MODEL), const),
                  pl.BlockSpec(memory_space=pl.ANY)],
        out_specs=pl.BlockSpec((MOE_TM, D_MODEL), row),
        scratch_shapes=[pltpu.VMEM((2, SORT_ROWS, D_MODEL), F32), pltpu.SemaphoreType.DMA((2,))],
    )
    return pl.pallas_call(
        _combine_kernel,
        grid_spec=grid_spec,
        out_shape=jax.ShapeDtypeStruct((rows, D_MODEL), F32),
        compiler_params=pltpu.CompilerParams(dimension_semantics=("arbitrary",), vmem_limit_bytes=VMEM_LIMIT),
        name="moe_combine",
    )(*tabs, h1, comb, g2, b2, ys)


def _moe_tables(cnt, rows):
    n_tiles = cnt.shape[0]
    up = lambda a, m: (a + m - 1) // m * m
    seg = up(cnt, SEG_ALIGN)
    soff = jnp.cumsum(seg, axis=1) - seg
    gtot = jnp.sum(seg, axis=0)
    gpad = up(gtot, MOE_TM)
    gbase = jnp.cumsum(gpad) - gpad
    doff = gbase[None, :] + jnp.cumsum(seg, axis=0) - seg
    n_ffn_tiles = -(-(rows + n_tiles * N_GROUPS * (SEG_ALIGN - 1)) // MOE_TM) + N_GROUPS
    starts = jnp.arange(n_ffn_tiles, dtype=jnp.int32) * MOE_TM
    tile_group = jnp.minimum(jnp.sum(starts[:, None] >= (gbase + gpad)[None, :], axis=1), N_GROUPS - 1)
    n_valid = (jnp.sum(gpad) // MOE_TM).reshape(1)
    i32 = lambda a: a.reshape(-1).astype(jnp.int32)
    return dict(seg=i32(seg), soff=i32(soff), doff=i32(doff), tlen=i32(gpad - gtot), toff=i32(gbase + gtot),
                tile_group=i32(tile_group), n_valid=i32(n_valid)), n_ffn_tiles


def _pad_lanes(a, col0=0):
    return jnp.pad(a, ((0, 0), (col0, LANES - col0 - a.shape[1])))


def kernel(x, meta_tokens, ln_emb_g, ln_emb_b, w_in, b_gate, dn_conv_w, dn_a_log, dn_dt_bias, dn_norm_g,
           w_branch_dn, w_branch_sb, w_out, ln1_g, ln1_b, router_group_w, router_group_b, router_expert_w,
           router_expert_b, expert_w_gate, expert_w_up, expert_w_down, ln2_g, ln2_b):
    batch, seq, d = x.shape
    assert d == D_MODEL and w_in.shape[0] == 1 and seq % max(GDN_BLOCK, SB_T) == 0 and batch % GDN_STREAMS == 0
    rows = batch * seq
    tm = 512
    assert rows % tm == 0
    row1 = lambda a: a.reshape(1, -1).astype(F32)

    w = w_in[0]
    c0 = 3 * DN_WIDTH
    c1 = c0 + DN_WIDTH
    c2 = c1 + 2 * DN_HEADS
    c3 = c2 + 3 * SB_WIDTH
    w_dn = w[:, :c0].astype(BF16)
    w_ba = _pad_lanes(w[:, c1:c2]).astype(BF16)
    w_sb = w[:, c2:c3].astype(BF16)
    w_zg = jnp.concatenate([w[:, c0:c1], w[:, c3:]], axis=1).astype(BF16)

    x2 = x.reshape(rows, d)
    g0, b0 = row1(ln_emb_g), row1(ln_emb_b)
    dn, sb, ba = _ln_proj(x2, g0, b0, w_dn, w_sb, w_ba, tm=tm)
    xm = jnp.concatenate([jnp.zeros((META_PAD, d), x.dtype), meta_tokens.astype(x.dtype)], axis=0)
    mdn, msb, mba = _ln_proj(xm, g0, b0, w_dn, w_sb, w_ba, tm=META_ROWS, n_zero=META_PAD)

    alog_row = _pad_lanes(dn_a_log[0].reshape(1, -1).astype(F32), DN_HEADS)
    dtb_row = _pad_lanes(dn_dt_bias[0].reshape(1, -1).astype(F32), DN_HEADS)
    o_dn = _gdn(dn, ba, mdn, mba, dn_conv_w[0].astype(F32), alog_row, dtb_row, row1(dn_norm_g[0]),
                batch=batch, seq=seq)
    o_sb = _sb_attn(sb, msb, batch=batch, seq=seq)

    w_r = _pad_lanes(jnp.concatenate(
        [router_group_w[0], router_expert_w[0].transpose(1, 0, 2).reshape(d, N_EXPERTS)], axis=1)).astype(BF16)
    b_r = _pad_lanes(jnp.concatenate(
        [router_group_b[0].reshape(1, -1), router_expert_b[0].reshape(1, -1)], axis=1).astype(F32))
    h1, comb, cnt = _mix_out(x2, o_dn, o_sb, g0, b0, w_zg, b_gate[0].astype(F32), w_branch_dn[0].astype(BF16),
                             w_branch_sb[0].astype(BF16), w_out[0].astype(BF16), row1(ln1_g[0]),
                             row1(ln1_b[0]), w_r, b_r, tm=MOE_TM)

    wg = expert_w_gate[0].astype(BF16)
    wu = expert_w_up[0].astype(BF16)
    wd = expert_w_down[0].reshape(N_GROUPS, EXPERTS_PER_GROUP * EXPERT_FF, d).astype(BF16)
    tabs, n_ffn_tiles = _moe_tables(cnt[:, 0, :N_GROUPS].astype(jnp.int32), rows)
    seg_tabs = (tabs["seg"], tabs["soff"], tabs["doff"])
    xg = _dispatch(h1, comb, seg_tabs + (tabs["tlen"], tabs["toff"], tabs["n_valid"]), n_ffn_tiles=n_ffn_tiles)
    ys = _group_ffn(xg, tabs["tile_group"], tabs["n_valid"], wg, wu, wd)
    out = _combine(h1, comb, ys, row1(ln2_g[0]), row1(ln2_b[0]), seg_tabs)
    return out.reshape(batch, seq, d)
```

```python
import functools

import jax
import jax.numpy as jnp
import numpy as np
from jax import lax
from jax.experimental import pallas as pl
from jax.experimental.pallas import tpu as pltpu

F32 = jnp.float32
BF16 = jnp.bfloat16

D_MODEL = 1024
N_META = 16
DN_HEADS = 4
DN_HEAD_DIM = 128
DN_WIDTH = DN_HEADS * DN_HEAD_DIM
DN_CONV = 4
DN_CHUNK = 64
SB_HEADS = 8
SB_HEAD_DIM = 64
SB_WIDTH = SB_HEADS * SB_HEAD_DIM
N_GROUPS = 4
EXPERTS_PER_GROUP = 8
N_EXPERTS = N_GROUPS * EXPERTS_PER_GROUP
EXPERT_FF = 256
DEEPNORM_ALPHA = 2.0 ** 0.25
LN_EPS = 1e-5
RMS_EPS = 1e-6

LANES = 128
SUBLANES = 8
META_ROWS = DN_CHUNK
META_PAD = META_ROWS - N_META
ROUTER_COL0 = N_GROUPS
VMEM_LIMIT = 56 * 1024 * 1024


def _layer_norm(x, g, b):
    mu = jnp.mean(x, -1, keepdims=True)
    xc = x - mu
    var = jnp.mean(xc * xc, -1, keepdims=True)
    return xc * lax.rsqrt(var + LN_EPS) * g + b


def _sigmoid(x):
    return 1.0 / (1.0 + jnp.exp(-x))


def _softplus(x):
    return jnp.maximum(x, 0.0) + jnp.log(1.0 + jnp.exp(-jnp.abs(x)))


def _silu(x):
    return x * _sigmoid(x)


def _dot(a, b):
    return jnp.dot(a, b, preferred_element_type=F32)


def _dot_nt(a, b):
    return lax.dot_general(a, b, (((1,), (1,)), ((), ())), preferred_element_type=F32)


def _dot_tn(a, b):
    return lax.dot_general(a, b, (((0,), (0,)), ((), ())), preferred_element_type=F32)


def _ln_proj_kernel(x_ref, g_ref, b_ref, wdn_ref, wsb_ref, wba_ref, dn_ref, sb_ref, ba_ref, *, n_zero):
    h = _layer_norm(x_ref[...], g_ref[...], b_ref[...])
    if n_zero:
        rows = lax.broadcasted_iota(jnp.int32, h.shape, 0)
        h = jnp.where(rows >= n_zero, h, 0.0)
    hb = h.astype(BF16)
    dn_ref[...] = _dot(hb, wdn_ref[...])
    sb_ref[...] = _dot(hb, wsb_ref[...]).astype(BF16)
    ba_ref[...] = _dot(hb, wba_ref[...])


def _ln_proj(x2, g, b, wdn, wsb, wba, *, tm, n_zero=0):
    rows = x2.shape[0]
    const = lambda i: (0, 0)
    row = lambda i: (i, 0)
    return pl.pallas_call(
        functools.partial(_ln_proj_kernel, n_zero=n_zero),
        grid=(rows // tm,),
        in_specs=[
            pl.BlockSpec((tm, D_MODEL), row),
            pl.BlockSpec((1, D_MODEL), const),
            pl.BlockSpec((1, D_MODEL), const),
            pl.BlockSpec(wdn.shape, const),
            pl.BlockSpec(wsb.shape, const),
            pl.BlockSpec(wba.shape, const),
        ],
        out_specs=[
            pl.BlockSpec((tm, 3 * DN_WIDTH), row),
            pl.BlockSpec((tm, 3 * SB_WIDTH), row),
            pl.BlockSpec((tm, LANES), row),
        ],
        out_shape=[
            jax.ShapeDtypeStruct((rows, 3 * DN_WIDTH), F32),
            jax.ShapeDtypeStruct((rows, 3 * SB_WIDTH), BF16),
            jax.ShapeDtypeStruct((rows, LANES), F32),
        ],
        compiler_params=pltpu.CompilerParams(dimension_semantics=("arbitrary",), vmem_limit_bytes=VMEM_LIMIT),
        name="ln_proj",
    )(x2, g, b, wdn, wsb, wba)


GDN_BLOCK = 256
GDN_STREAMS = 1
GDN_CHUNK = DN_CHUNK
CONV_HIST = 8


def _gdn_rows(streams, n, n_zero, cw_ref, alog_ref, dtb_ref, ng_ref):
    c = min(n, GDN_CHUNK)
    ri = lax.broadcasted_iota(jnp.int32, (n, n), 0)
    ci = lax.broadcasted_iota(jnp.int32, (n, n), 1)
    same = (ri // c) == (ci // c)
    causal = same & (ri >= ci)
    strict = same & (ri > ci)
    tril = causal.astype(BF16)

    pre = []
    for src_ref, ba_ref, _, xbuf, _, vn_ref in streams:
        xbuf[CONV_HIST:CONV_HIST + n, :] = src_ref[...]
        acc = xbuf[CONV_HIST:CONV_HIST + n, :] * cw_ref[DN_CONV - 1:DN_CONV, :]
        for i in range(DN_CONV - 1):
            s = DN_CONV - 1 - i
            acc = acc + xbuf[CONV_HIST - s:CONV_HIST - s + n, :] * cw_ref[i:i + 1, :]
        hist = xbuf[n:n + CONV_HIST, :]
        xbuf[0:CONV_HIST, :] = hist
        qkv = _silu(acc)

        ba = ba_ref[...]
        beta_all = _sigmoid(ba)
        g_all = -jnp.exp(alog_ref[...]) * _softplus(ba + dtb_ref[...])
        if n_zero:
            rows = lax.broadcasted_iota(jnp.int32, g_all.shape, 0)
            g_all = jnp.where(rows >= n_zero, g_all, 0.0)
        g_hi = g_all.astype(BF16)
        g_lo = (g_all - g_hi.astype(F32)).astype(BF16)
        dec = _dot(tril, g_hi) + _dot(tril, g_lo)
        vn_ref[...] = jnp.zeros_like(vn_ref)
        pre.append((qkv, beta_all, dec, dec.T))

    units = [(si, h) for si in range(len(streams)) for h in range(DN_HEADS)]
    qe, ks, d_cols, qks, rhss, ps = [], [], [], [], [], []
    for si, h in units:
        qkv, beta_all, dec, dec_t = pre[si]
        q = qkv[:, h * DN_HEAD_DIM:(h + 1) * DN_HEAD_DIM]
        k = qkv[:, DN_WIDTH + h * DN_HEAD_DIM:DN_WIDTH + (h + 1) * DN_HEAD_DIM]
        v = qkv[:, 2 * DN_WIDTH + h * DN_HEAD_DIM:2 * DN_WIDTH + (h + 1) * DN_HEAD_DIM]
        q = q * lax.rsqrt(jnp.sum(q * q, -1, keepdims=True) + RMS_EPS) * (DN_HEAD_DIM ** -0.5)
        k = k * lax.rsqrt(jnp.sum(k * k, -1, keepdims=True) + RMS_EPS)
        beta = beta_all[:, h:h + 1]
        d_col = dec[:, DN_HEADS + h:DN_HEADS + h + 1]
        d_row = dec_t[DN_HEADS + h:DN_HEADS + h + 1, :]
        lmask = jnp.where(causal, jnp.exp(jnp.where(causal, d_col - d_row, 0.0)), 0.0)
        kb = k * beta
        k16 = k.astype(BF16)
        a = jnp.where(strict, _dot_nt(kb.astype(BF16), k16) * lmask, 0.0)
        qks.append((_dot_nt(q.astype(BF16), k16) * lmask).astype(BF16))
        e_col = jnp.exp(d_col)
        rhss.append(jnp.concatenate([v * beta, kb * e_col], axis=1))
        qe.append(q * e_col)
        ks.append(k)
        d_cols.append(d_col)
        ps.append(-a)
    ts = list(ps)
    for _ in range(c.bit_length() - 2):
        p16s = [p.astype(BF16) for p in ps]
        ps = [_dot(p16, p16) for p16 in p16s]
        ts = [t + p + _dot(t.astype(BF16), p.astype(BF16)) for t, p in zip(ts, ps)]
    uws = [rhs + _dot(t.astype(BF16), rhs.astype(BF16)) for t, rhs in zip(ts, rhss)]

    for ch in range(n // c):
        r0 = ch * c
        for u, (si, h) in enumerate(units):
            _, _, o_ref, _, s_ref, vn_ref = streams[si]
            k, d_col, qk, uw = ks[u], d_cols[u], qks[u], uws[u]
            d_c = d_col[r0:r0 + c, :]
            d_last = d_col[r0 + c - 1:r0 + c, :]
            s = s_ref[h]
            wq = jnp.concatenate([uw[r0:r0 + c, DN_HEAD_DIM:], qe[u][r0:r0 + c, :]], axis=0)
            r = _dot(wq.astype(BF16), s.astype(BF16))
            v_new = uw[r0:r0 + c, :DN_HEAD_DIM] - r[:c, :]
            vn16 = v_new.astype(BF16)
            vn_ref[h, r0:r0 + c, :] = vn16
            o = r[c:, :] + _dot(qk[r0:r0 + c, :], vn_ref[h])
            k_dec = k[r0:r0 + c, :] * jnp.exp(d_last - d_c)
            s_ref[h] = s * jnp.exp(d_last) + _dot_tn(k_dec.astype(BF16), vn16)
            if o_ref is not None:
                o_n = o * lax.rsqrt(jnp.mean(o * o, -1, keepdims=True) + RMS_EPS) * ng_ref[...]
                o_ref[r0:r0 + c, h * DN_HEAD_DIM:(h + 1) * DN_HEAD_DIM] = o_n


def _gdn_kernel(dn_ref, ba_ref, mdn_ref, mba_ref, cw_ref, alog_ref, dtb_ref, ng_ref, o_ref, xbuf, s_ref, vn_ref,
                vnm_ref):
    @pl.when(pl.program_id(1) == 0)
    def _():
        s_ref[0] = jnp.zeros(s_ref.shape[1:], F32)
        xbuf[0, 0:CONV_HIST, :] = jnp.zeros((CONV_HIST, xbuf.shape[2]), F32)
        _gdn_rows([(mdn_ref, mba_ref, None, xbuf.at[0], s_ref.at[0], vnm_ref)], META_ROWS, META_PAD,
                  cw_ref, alog_ref, dtb_ref, ng_ref)
        for si in range(1, GDN_STREAMS):
            s_ref[si] = s_ref[0]
            xbuf[si, 0:CONV_HIST, :] = xbuf[0, 0:CONV_HIST, :]

    _gdn_rows([(dn_ref.at[si], ba_ref.at[si], o_ref.at[si], xbuf.at[si], s_ref.at[si], vn_ref.at[si])
               for si in range(GDN_STREAMS)], GDN_BLOCK, 0, cw_ref, alog_ref, dtb_ref, ng_ref)


def _gdn(dn, ba, mdn, mba, conv_w, alog_row, dtb_row, norm_g, *, batch, seq):
    nb = seq // GDN_BLOCK
    const = lambda b, j: (0, 0)
    row = lambda b, j: (b, j, 0)
    dn = dn.reshape(batch, seq, dn.shape[-1])
    ba = ba.reshape(batch, seq, ba.shape[-1])
    out = pl.pallas_call(
        _gdn_kernel,
        grid=(batch // GDN_STREAMS, nb),
        in_specs=[
            pl.BlockSpec((GDN_STREAMS, GDN_BLOCK, 3 * DN_WIDTH), row),
            pl.BlockSpec((GDN_STREAMS, GDN_BLOCK, LANES), row),
            pl.BlockSpec((META_ROWS, 3 * DN_WIDTH), const),
            pl.BlockSpec((META_ROWS, LANES), const),
            pl.BlockSpec((DN_CONV, 3 * DN_WIDTH), const),
            pl.BlockSpec((1, LANES), const),
            pl.BlockSpec((1, LANES), const),
            pl.BlockSpec((1, DN_HEAD_DIM), const),
        ],
        out_specs=pl.BlockSpec((GDN_STREAMS, GDN_BLOCK, DN_WIDTH), row),
        out_shape=jax.ShapeDtypeStruct((batch, seq, DN_WIDTH), F32),
        scratch_shapes=[
            pltpu.VMEM((GDN_STREAMS, GDN_BLOCK + CONV_HIST, 3 * DN_WIDTH), F32),
            pltpu.VMEM((GDN_STREAMS, DN_HEADS, DN_HEAD_DIM, DN_HEAD_DIM), F32),
            pltpu.VMEM((GDN_STREAMS, DN_HEADS, GDN_BLOCK, DN_HEAD_DIM), BF16),
            pltpu.VMEM((DN_HEADS, META_ROWS, DN_HEAD_DIM), BF16),
        ],
        compiler_params=pltpu.CompilerParams(dimension_semantics=("arbitrary", "arbitrary"),
                                             vmem_limit_bytes=VMEM_LIMIT),
        name="gdn",
    )(dn, ba, mdn, mba, conv_w, alog_row, dtb_row, norm_g)
    return out.reshape(batch * seq, DN_WIDTH)


SB_T = 2 * LANES
HEADS_PER_BLOCK = LANES // SB_HEAD_DIM
SB_SLOTS = 4
SB_UNROLL = 16
NEG_BIG = -1e30


def _sb_schedule(nq):
    qoff, koff, bsel, first = [], [], [], []
    for qi in range(nq):
        for kj in range(qi, -1, -1):
            qoff.append(qi * SB_T)
            koff.append(kj * SB_T)
            bsel.append(1 if kj == qi else 0)
            first.append(1 if kj == qi else 0)
    return [np.asarray(a, np.int32) for a in (qoff, koff, bsel, first)]


def _sb_kernel(qoff_ref, koff_ref, bsel_ref, first_ref, q_ref, k_ref, v_ref, mk_ref, mv_ref, o_ref,
               qm_s, bias_s, nu_s, z_s, sp_s, later_s, w_s, acc_s, carry_s, accq_s, carryq_s, *, n_tiles):
    t = SB_T
    ri = lax.broadcasted_iota(jnp.int32, (t, t), 0)
    ci = lax.broadcasted_iota(jnp.int32, (t, t), 1)
    bias_s[0] = jnp.zeros((t, t), F32)
    bias_s[1] = jnp.where(ci < ri, 0.0, NEG_BIG)
    acc_s[...] = jnp.zeros_like(acc_s)
    carry_s[...] = jnp.zeros_like(carry_s)
    nu_s[...] = jnp.where(ri > ci, -1.0, 0.0).astype(BF16)
    lane = lax.broadcasted_iota(jnp.int32, (t, LANES), 1)
    scale = SB_HEAD_DIM ** -0.5

    q_all = q_ref[...]
    lane_q = lax.broadcasted_iota(jnp.int32, q_all.shape, 1)
    for h in range(HEADS_PER_BLOCK):
        qm_s[h] = jnp.where((lane_q // SB_HEAD_DIM) == h, q_all, jnp.zeros_like(q_all)) * scale

    def st_logits(n, s):
        qo = pl.multiple_of(qoff_ref[n], t)
        ko = pl.multiple_of(koff_ref[n], t)
        kt = k_ref[pl.ds(ko, t), :]
        bias = bias_s[bsel_ref[n]]
        for h in range(HEADS_PER_BLOCK):
            z_s[s, h] = _dot_nt(qm_s[h, pl.ds(qo, t), :], kt) + bias

    def st_softplus(n, s):
        for h in range(HEADS_PER_BLOCK):
            z = z_s[s, h]
            sp = jnp.maximum(z, 0.0) + jnp.log(1.0 + jnp.exp(-jnp.abs(z)))
            sp_s[s, h] = sp.astype(BF16)
            z_s[s, h] = z - sp

    def st_keysum(n, s):
        for h in range(HEADS_PER_BLOCK):
            later_s[s, h] = _dot(sp_s[s, h], nu_s[...])

    def st_weights(n, s):
        keep = jnp.where(first_ref[n] == 1, 0.0, 1.0)
        qb = qoff_ref[n] // t
        for h in range(HEADS_PER_BLOCK):
            later = later_s[s, h]
            carry = carry_s[h] * keep
            logw = z_s[s, h] + later + jnp.concatenate([carry] * (t // LANES), axis=1)
            w_s[s, h] = jnp.exp(logw).astype(BF16)
            total = later[:, 0:1] - sp_s[s, h, :, 0:1].astype(F32)
            carry = carry + jnp.broadcast_to(total, (t, LANES))
            carry_s[h] = carry
            carryq_s[qb, h] = carry

    def st_values(n, s):
        ko = pl.multiple_of(koff_ref[n], t)
        vt = v_ref[pl.ds(ko, t), :]
        keep = jnp.where(first_ref[n] == 1, 0.0, 1.0)
        qb = qoff_ref[n] // t
        for h in range(HEADS_PER_BLOCK):
            acc = acc_s[h] * keep + _dot(w_s[s, h], vt)
            acc_s[h] = acc
            accq_s[qb, h] = acc

    stages = (st_logits, st_softplus, st_keysum, st_weights, st_values)
    depth = len(stages)

    def trip(it, parity):
        for d in reversed(range(depth)):
            n = it - d
            if isinstance(n, int) and not 0 <= n < n_tiles:
                continue
            stages[d](n, (parity - d) % SB_SLOTS)

    for it in range(depth - 1):
        trip(it, it % SB_SLOTS)
    first_full = depth - 1
    n_full = n_tiles - first_full
    n_loop = n_full // SB_UNROLL

    def body(u, c):
        base = first_full + u * SB_UNROLL
        for j in range(SB_UNROLL):
            trip(base + j, (first_full + j) % SB_SLOTS)
        return c

    lax.fori_loop(0, n_loop, body, 0)
    for it in range(first_full + n_loop * SB_UNROLL, n_tiles + depth - 1):
        trip(it, it % SB_SLOTS)

    m = META_ROWS
    mrow = lax.broadcasted_iota(jnp.int32, (HEADS_PER_BLOCK * m, LANES), 0)
    mlane = lax.broadcasted_iota(jnp.int32, (HEADS_PER_BLOCK * m, LANES), 1)
    own = (mrow // m) == (mlane // SB_HEAD_DIM)
    mkk = jnp.where(own, jnp.concatenate([mk_ref[...]] * HEADS_PER_BLOCK, axis=0), jnp.zeros((), BF16))
    mvv = jnp.where(own, jnp.concatenate([mv_ref[...]] * HEADS_PER_BLOCK, axis=0), jnp.zeros((), BF16))
    mr = lax.broadcasted_iota(jnp.int32, (LANES, LANES), 0)
    mc = lax.broadcasted_iota(jnp.int32, (LANES, LANES), 1)
    nu_m = jnp.where((mr > mc) & ((mr // m) == (mc // m)), -1.0, 0.0).astype(BF16)
    mbias = jnp.where((lane % m) >= META_PAD, 0.0, NEG_BIG)
    head1 = (lane // SB_HEAD_DIM) == 1
    for qb in range(q_ref.shape[0] // t):
        rows = slice(qb * t, (qb + 1) * t)
        z = _dot_nt(q_ref[rows, :] * scale, mkk) + mbias
        sp = jnp.maximum(z, 0.0) + jnp.log(1.0 + jnp.exp(-jnp.abs(z)))
        later = _dot(sp.astype(BF16), nu_m)
        carry = jnp.where(head1, carryq_s[qb, 1], carryq_s[qb, 0])
        w = jnp.exp(z - sp + later + carry)
        acc = jnp.where(head1, accq_s[qb, 1], accq_s[qb, 0]) + _dot(w.astype(BF16), mvv)
        o_ref[rows, :] = acc.astype(o_ref.dtype)


def _sb_attn(sb, msb, *, batch, seq):
    nq = seq // SB_T
    n_hb = SB_WIDTH // LANES
    sched = _sb_schedule(nq)
    n_tiles = int(sched[0].shape[0])
    grid_spec = pltpu.PrefetchScalarGridSpec(
        num_scalar_prefetch=len(sched),
        grid=(batch, n_hb),
        in_specs=[
            pl.BlockSpec((seq, LANES), lambda b, hp, *_: (b, hp)),
            pl.BlockSpec((seq, LANES), lambda b, hp, *_: (b, n_hb + hp)),
            pl.BlockSpec((seq, LANES), lambda b, hp, *_: (b, 2 * n_hb + hp)),
            pl.BlockSpec((META_ROWS, LANES), lambda b, hp, *_: (0, n_hb + hp)),
            pl.BlockSpec((META_ROWS, LANES), lambda b, hp, *_: (0, 2 * n_hb + hp)),
        ],
        out_specs=pl.BlockSpec((seq, LANES), lambda b, hp, *_: (b, hp)),
        scratch_shapes=[
            pltpu.VMEM((HEADS_PER_BLOCK, seq, LANES), BF16),
            pltpu.VMEM((2, SB_T, SB_T), F32),
            pltpu.VMEM((SB_T, SB_T), BF16),
            pltpu.VMEM((SB_SLOTS, HEADS_PER_BLOCK, SB_T, SB_T), F32),
            pltpu.VMEM((SB_SLOTS, HEADS_PER_BLOCK, SB_T, SB_T), BF16),
            pltpu.VMEM((SB_SLOTS, HEADS_PER_BLOCK, SB_T, SB_T), F32),
            pltpu.VMEM((SB_SLOTS, HEADS_PER_BLOCK, SB_T, SB_T), BF16),
            pltpu.VMEM((HEADS_PER_BLOCK, SB_T, LANES), F32),
            pltpu.VMEM((HEADS_PER_BLOCK, SB_T, LANES), F32),
            pltpu.VMEM((nq, HEADS_PER_BLOCK, SB_T, LANES), F32),
            pltpu.VMEM((nq, HEADS_PER_BLOCK, SB_T, LANES), F32),
        ],
    )
    return pl.pallas_call(
        functools.partial(_sb_kernel, n_tiles=n_tiles),
        grid_spec=grid_spec,
        out_shape=jax.ShapeDtypeStruct((batch * seq, SB_WIDTH), BF16),
        compiler_params=pltpu.CompilerParams(dimension_semantics=("arbitrary", "arbitrary"),
                                             vmem_limit_bytes=VMEM_LIMIT),
        name="sb_attn",
    )(*[jnp.asarray(a) for a in sched], sb, sb, sb, msb, msb)


MIX_CHAINS = 4


def _masked_lane_max(x, mask):
    return jnp.max(jnp.where(mask, x, -jnp.inf), -1, keepdims=True)


def _first_lane_eq(x, val, mask, lane):
    return jnp.min(jnp.where(mask & (x == val), lane, LANES), -1, keepdims=True)


def _route(logits):
    lane = lax.broadcasted_iota(jnp.int32, logits.shape, 1)
    gmask = lane < N_GROUPS
    gmax = _masked_lane_max(logits, gmask)
    g_idx = _first_lane_eq(logits, gmax, gmask, lane)
    g_prob = 1.0 / jnp.sum(jnp.where(gmask, jnp.exp(logits - gmax), 0.0), -1, keepdims=True)
    lo = ROUTER_COL0 + g_idx * EXPERTS_PER_GROUP
    emask = (lane >= lo) & (lane < lo + EXPERTS_PER_GROUP)
    t1 = _masked_lane_max(logits, emask)
    i1 = _first_lane_eq(logits, t1, emask, lane)
    emask2 = emask & (lane != i1)
    t2 = _masked_lane_max(logits, emask2)
    i2 = _first_lane_eq(logits, t2, emask2, lane)
    e = jnp.exp(t2 - t1)
    w1 = g_prob / (1.0 + e)
    w2 = g_prob * e / (1.0 + e)
    comb = jnp.where(lane == i1, w1, 0.0) + jnp.where(lane == i2, w2, 0.0)
    comb = jnp.where(lane == 0, g_idx.astype(F32), comb)
    counts = jnp.sum((lane == g_idx).astype(F32), 0, keepdims=True)
    return comb, counts


def _mix_out_kernel(x_ref, odn_ref, osb_ref, g0_ref, b0_ref, wzg_ref, bg_ref, wbdn_ref, wbsb_ref, wout_ref,
                    g1_ref, b1_ref, wr_ref, br_ref, h1_ref, comb_ref, cnt_ref):
    sub = x_ref.shape[0] // MIX_CHAINS
    rows = [slice(r0, r0 + sub) for r0 in range(0, x_ref.shape[0], sub)]
    h0s = [_layer_norm(x_ref[rs, :], g0_ref[...], b0_ref[...]) for rs in rows]
    zgs = [_dot(h0.astype(BF16), wzg_ref[...]) for h0 in h0s]
    bs = [_dot(osb_ref[rs, :], wbsb_ref[...]) for rs in rows]
    o_dns = [odn_ref[rs, :] * _silu(zg[:, :DN_WIDTH]) for rs, zg in zip(rows, zgs)]
    as_ = [_dot(o_dn.astype(BF16), wbdn_ref[...]) for o_dn in o_dns]
    merged = [_sigmoid(zg[:, DN_WIDTH:DN_WIDTH + D_MODEL] + bg_ref[0:1, :]) * a
              + _sigmoid(zg[:, DN_WIDTH + D_MODEL:] + bg_ref[1:2, :]) * b for zg, a, b in zip(zgs, as_, bs)]
    mixes = [_dot(m.astype(BF16), wout_ref[...]) for m in merged]
    h1s = [_layer_norm(DEEPNORM_ALPHA * h0 + mix, g1_ref[...], b1_ref[...]) for h0, mix in zip(h0s, mixes)]
    logits = [_dot(h1.astype(BF16), wr_ref[...]) + br_ref[...] for h1 in h1s]
    counts = jnp.zeros((1, LANES), F32)
    for rs, h1, lg in zip(rows, h1s, logits):
        h1_ref[rs, :] = h1
        comb, cnt = _route(lg)
        comb_ref[rs, :] = comb
        counts = counts + cnt
    cnt_ref[0] = jnp.broadcast_to(counts, cnt_ref.shape[1:])


def _mix_out(x2, o_dn, o_sb, g0, b0, wzg, bg, wbdn, wbsb, wout, g1, b1, wr, br, *, tm):
    rows = x2.shape[0]
    const = lambda i: (0, 0)
    row = lambda i: (i, 0)
    full = lambda a: pl.BlockSpec(a.shape, const)
    return pl.pallas_call(
        _mix_out_kernel,
        grid=(rows // tm,),
        in_specs=[
            pl.BlockSpec((tm, D_MODEL), row),
            pl.BlockSpec((tm, DN_WIDTH), row),
            pl.BlockSpec((tm, SB_WIDTH), row),
            full(g0), full(b0), full(wzg), full(bg), full(wbdn), full(wbsb), full(wout), full(g1), full(b1),
            full(wr), full(br),
        ],
        out_specs=[pl.BlockSpec((tm, D_MODEL), row), pl.BlockSpec((tm, LANES), row),
                   pl.BlockSpec((1, SUBLANES, LANES), lambda i: (i, 0, 0))],
        out_shape=[jax.ShapeDtypeStruct((rows, D_MODEL), F32), jax.ShapeDtypeStruct((rows, LANES), F32),
                   jax.ShapeDtypeStruct((rows // tm, SUBLANES, LANES), F32)],
        compiler_params=pltpu.CompilerParams(dimension_semantics=("arbitrary",), vmem_limit_bytes=VMEM_LIMIT),
        name="mix_out",
    )(x2, o_dn, o_sb, g0, b0, wzg, bg, wbdn, wbsb, wout, g1, b1, wr, br)


MOE_TM = 512
SEG_ALIGN = 16
SEG_SIZES = (512, 256, 128, 64, 32, 16)
SORT_ROWS = MOE_TM + 64
X_EXT = D_MODEL + 2 * LANES


def _sorted_one_hot(comb, soff_ref, i):
    tm = comb.shape[0]
    lane = lax.broadcasted_iota(jnp.int32, (tm, LANES), 1)
    onehot = lane == comb[:, 0:1].astype(jnp.int32)
    ri = lax.broadcasted_iota(jnp.int32, (tm, tm), 0)
    ci = lax.broadcasted_iota(jnp.int32, (tm, tm), 1)
    ranks = _dot((ri > ci).astype(BF16), onehot.astype(BF16))
    start = jnp.zeros((1, LANES), F32)
    for g in range(N_GROUPS):
        start = jnp.where(lane[0:1, :] == g, soff_ref[i * N_GROUPS + g].astype(F32), start)
    pos = jnp.sum(jnp.where(onehot, ranks + start, 0.0), -1, keepdims=True)
    slot = lax.broadcasted_iota(jnp.int32, (tm, SORT_ROWS), 1)
    return (slot == pos.astype(jnp.int32)).astype(BF16)


def _for_each_piece(length, fn):
    for size in SEG_SIZES:
        @pl.when((length & size) != 0)
        def _(size=size):
            fn(pl.multiple_of(length & (-2 * size), SEG_ALIGN), size)


def _segment_copies(i, len_ref, src_ref, soff_ref, dst_ref, doff_ref, sem, act):
    for g in range(N_GROUPS):
        n = i * N_GROUPS + g
        so = soff_ref[n]
        do = doff_ref[n]

        def piece(off, size, so=so, do=do):
            act(pltpu.make_async_copy(src_ref.at[pl.ds(pl.multiple_of(so + off, SEG_ALIGN), size)],
                                      dst_ref.at[pl.ds(pl.multiple_of(do + off, SEG_ALIGN), size)], sem))

        _for_each_piece(len_ref[n], piece)


def _dispatch_kernel(len_ref, soff_ref, doff_ref, tlen_ref, toff_ref, nv_ref, h1_ref, comb_ref, xg_ref, xs_ref,
                     zero_ref, sem):
    i = pl.program_id(0)
    comb = comb_ref[...]
    pt = _sorted_one_hot(comb, soff_ref, i)
    lane = lax.broadcasted_iota(jnp.int32, comb.shape, 1)
    c = jnp.where(lane >= ROUTER_COL0, comb, 0.0)
    c_hi = c.astype(BF16)
    c_lo = (c - c_hi.astype(F32)).astype(BF16)
    src = jnp.concatenate([h1_ref[...].astype(BF16), c_hi, c_lo], axis=1)
    slot = i % 2
    xs_ref[slot] = _dot_tn(pt, src).astype(BF16)

    def to_groups(tile, sl, act):
        _segment_copies(tile, len_ref, xs_ref.at[sl], soff_ref, xg_ref, doff_ref, sem.at[sl], act)

    to_groups(i, slot, lambda cp: cp.start())

    @pl.when(i > 0)
    def _():
        to_groups(i - 1, 1 - slot, lambda cp: cp.wait())

    @pl.when(i == pl.num_programs(0) - 1)
    def _():
        fill_sem = sem.at[2]
        zero_ref[...] = jnp.zeros_like(zero_ref)
        for act in (lambda cp: cp.start(), lambda cp: cp.wait()):
            for g in range(N_GROUPS):
                def piece(off, size, g=g, act=act):
                    act(pltpu.make_async_copy(
                        zero_ref.at[pl.ds(off, size)],
                        xg_ref.at[pl.ds(pl.multiple_of(toff_ref[g] + off, SEG_ALIGN), size)], fill_sem))

                _for_each_piece(tlen_ref[g], piece)

        def zero_tile(k, carry):
            cp = pltpu.make_async_copy(zero_ref, xg_ref.at[pl.ds(pl.multiple_of(k * MOE_TM, MOE_TM), MOE_TM)],
                                       fill_sem)
            cp.start()
            cp.wait()
            return carry

        lax.fori_loop(nv_ref[0], xg_ref.shape[0] // MOE_TM, zero_tile, 0)
        to_groups(i, slot, lambda cp: cp.wait())


def _dispatch(h1, comb, tabs, *, n_ffn_tiles):
    rows = h1.shape[0]
    row = lambda i, *_: (i, 0)
    grid_spec = pltpu.PrefetchScalarGridSpec(
        num_scalar_prefetch=len(tabs),
        grid=(rows // MOE_TM,),
        in_specs=[pl.BlockSpec((MOE_TM, D_MODEL), row), pl.BlockSpec((MOE_TM, LANES), row)],
        out_specs=pl.BlockSpec(memory_space=pl.ANY),
        scratch_shapes=[pltpu.VMEM((2, SORT_ROWS, X_EXT), BF16), pltpu.VMEM((MOE_TM, X_EXT), BF16),
                        pltpu.SemaphoreType.DMA((3,))],
    )
    return pl.pallas_call(
        _dispatch_kernel,
        grid_spec=grid_spec,
        out_shape=jax.ShapeDtypeStruct((n_ffn_tiles * MOE_TM, X_EXT), BF16),
        compiler_params=pltpu.CompilerParams(dimension_semantics=("arbitrary",), vmem_limit_bytes=VMEM_LIMIT),
        name="moe_dispatch",
    )(*tabs, h1, comb)


def _group_ffn_kernel(tg_ref, nv_ref, x_ref, wg_ref, wu_ref, wd_ref, y_ref):
    k = pl.program_id(0)

    @pl.when(k < nv_ref[0])
    def _():
        g = tg_ref[k]
        xe = x_ref[...]
        hb = xe[:, :D_MODEL]
        comb = xe[:, D_MODEL:D_MODEL + LANES].astype(F32) + xe[:, D_MODEL + LANES:].astype(F32)
        lane = lax.broadcasted_iota(jnp.int32, comb.shape, 1)
        parts = []
        for e in range(EXPERTS_PER_GROUP):
            col = ROUTER_COL0 + g * EXPERTS_PER_GROUP + e
            c = jnp.sum(jnp.where(lane == col, comb, 0.0), -1, keepdims=True)
            parts.append(_silu(_dot(hb, wg_ref[0, e])) * _dot(hb, wu_ref[0, e]) * c)
        y_ref[...] = _dot(jnp.concatenate(parts, axis=-1).astype(BF16), wd_ref[0])

    @pl.when(k >= nv_ref[0])
    def _():
        y_ref[...] = jnp.zeros_like(y_ref)


def _group_ffn(xg, tile_group, n_valid, wg, wu, wd):
    n_tiles = xg.shape[0] // MOE_TM
    row = lambda k, tg, nv: (jnp.minimum(k, nv[0] - 1), 0)
    out_row = lambda k, tg, nv: (k, 0)
    grp = lambda k, tg, nv: (tg[k], 0, 0)
    grid_spec = pltpu.PrefetchScalarGridSpec(
        num_scalar_prefetch=2,
        grid=(n_tiles,),
        in_specs=[
            pl.BlockSpec((MOE_TM, X_EXT), row),
            pl.BlockSpec((1,) + wg.shape[1:], lambda k, tg, nv: (tg[k], 0, 0, 0)),
            pl.BlockSpec((1,) + wu.shape[1:], lambda k, tg, nv: (tg[k], 0, 0, 0)),
            pl.BlockSpec((1,) + wd.shape[1:], grp),
        ],
        out_specs=pl.BlockSpec((MOE_TM, D_MODEL), out_row),
    )
    return pl.pallas_call(
        _group_ffn_kernel,
        grid_spec=grid_spec,
        out_shape=jax.ShapeDtypeStruct((n_tiles * MOE_TM, D_MODEL), F32),
        compiler_params=pltpu.CompilerParams(dimension_semantics=("arbitrary",), vmem_limit_bytes=VMEM_LIMIT),
        name="moe_ffn",
    )(tile_group, n_valid, xg, wg, wu, wd)


def _combine_kernel(len_ref, soff_ref, doff_ref, h1_ref, comb_ref, g2_ref, b2_ref, ys_ref, o_ref, ysb_ref, sem):
    i = pl.program_id(0)
    n = pl.num_programs(0)

    def fetch(tile, slot, act):
        _segment_copies(tile, len_ref, ys_ref, doff_ref, ysb_ref.at[slot], soff_ref, sem.at[slot], act)

    def start_fetch(tile, slot):
        ysb_ref[slot] = jnp.zeros(ysb_ref.shape[1:], F32)
        fetch(tile, slot, lambda cp: cp.start())

    @pl.when(i == 0)
    def _():
        start_fetch(0, 0)

    for slot in range(2):
        @pl.when((i + 1 < n) & ((i + 1) % 2 == slot))
        def _(slot=slot):
            start_fetch(i + 1, slot)

    pt = _sorted_one_hot(comb_ref[...], soff_ref, i)
    for slot in range(2):
        @pl.when(i % 2 == slot)
        def _(slot=slot):
            fetch(i, slot, lambda cp: cp.wait())

    ys = ysb_ref[i % 2]
    hi = ys.astype(BF16)
    lo = (ys - hi.astype(F32)).astype(BF16)
    ffn = _dot(pt, hi) + _dot(pt, lo)
    o_ref[...] = _layer_norm(DEEPNORM_ALPHA * h1_ref[...] + ffn, g2_ref[...], b2_ref[...])


def _combine(h1, comb, ys, g2, b2, tabs):
    rows = h1.shape[0]
    row = lambda i, *_: (i, 0)
    const = lambda i, *_: (0, 0)
    grid_spec = pltpu.PrefetchScalarGridSpec(
        num_scalar_prefetch=len(tabs),
        grid=(rows // MOE_TM,),
        in_specs=[pl.BlockSpec((MOE_TM, D_MODEL), row), pl.BlockSpec((MOE_TM, LANES), row),
                  pl.BlockSpec((1, D_MODEL), const), pl.BlockSpec((1, D_MODEL), const),
                  pl.BlockSpec(memory_space=pl.ANY)],
        out_specs=pl.BlockSpec((MOE_TM, D_MODEL), row),
        scratch_shapes=[pltpu.VMEM((2, SORT_ROWS, D_MODEL), F32), pltpu.SemaphoreType.DMA((2,))],
    )
    return pl.pallas_call(
        _combine_kernel,
        grid_spec=grid_spec,
        out_shape=jax.ShapeDtypeStruct((rows, D_MODEL), F32),
        compiler_params=pltpu.CompilerParams(dimension_semantics=("arbitrary",), vmem_limit_bytes=VMEM_LIMIT),
        name="moe_combine",
    )(*tabs, h1, comb, g2, b2, ys)


def _moe_tables(cnt, rows):
    n_tiles = cnt.shape[0]
    up = lambda a, m: (a + m - 1) // m * m
    seg = up(cnt, SEG_ALIGN)
    soff = jnp.cumsum(seg, axis=1) - seg
    gtot = jnp.sum(seg, axis=0)
    gpad = up(gtot, MOE_TM)
    gbase = jnp.cumsum(gpad) - gpad
    doff = gbase[None, :] + jnp.cumsum(seg, axis=0) - seg
    n_ffn_tiles = -(-(rows + n_tiles * N_GROUPS * (SEG_ALIGN - 1)) // MOE_TM) + N_GROUPS
    starts = jnp.arange(n_ffn_tiles, dtype=jnp.int32) * MOE_TM
    tile_group = jnp.minimum(jnp.sum(starts[:, None] >= (gbase + gpad)[None, :], axis=1), N_GROUPS - 1)
    n_valid = (jnp.sum(gpad) // MOE_TM).reshape(1)
    i32 = lambda a: a.reshape(-1).astype(jnp.int32)
    return dict(seg=i32(seg), soff=i32(soff), doff=i32(doff), tlen=i32(gpad - gtot), toff=i32(gbase + gtot),
                tile_group=i32(tile_group), n_valid=i32(n_valid)), n_ffn_tiles


def _pad_lanes(a, col0=0):
    return jnp.pad(a, ((0, 0), (col0, LANES - col0 - a.shape[1])))


def kernel(x, meta_tokens, ln_emb_g, ln_emb_b, w_in, b_gate, dn_conv_w, dn_a_log, dn_dt_bias, dn_norm_g,
           w_branch_dn, w_branch_sb, w_out, ln1_g, ln1_b, router_group_w, router_group_b, router_expert_w,
           router_expert_b, expert_w_gate, expert_w_up, expert_w_down, ln2_g, ln2_b):
    batch, seq, d = x.shape
    assert d == D_MODEL and w_in.shape[0] == 1 and seq % max(GDN_BLOCK, SB_T) == 0 and batch % GDN_STREAMS == 0
    rows = batch * seq
    tm = 512
    assert rows % tm == 0
    row1 = lambda a: a.reshape(1, -1).astype(F32)

    w = w_in[0]
    c0 = 3 * DN_WIDTH
    c1 = c0 + DN_WIDTH
    c2 = c1 + 2 * DN_HEADS
    c3 = c2 + 3 * SB_WIDTH
    w_dn = w[:, :c0].astype(BF16)
    w_ba = _pad_lanes(w[:, c1:c2]).astype(BF16)
    w_sb = w[:, c2:c3].astype(BF16)
    w_zg = jnp.concatenate([w[:, c0:c1], w[:, c3:]], axis=1).astype(BF16)

    x2 = x.reshape(rows, d)
    g0, b0 = row1(ln_emb_g), row1(ln_emb_b)
    dn, sb, ba = _ln_proj(x2, g0, b0, w_dn, w_sb, w_ba, tm=tm)
    xm = jnp.concatenate([jnp.zeros((META_PAD, d), x.dtype), meta_tokens.astype(x.dtype)], axis=0)
    mdn, msb, mba = _ln_proj(xm, g0, b0, w_dn, w_sb, w_ba, tm=META_ROWS, n_zero=META_PAD)

    alog_row = _pad_lanes(dn_a_log[0].reshape(1, -1).astype(F32), DN_HEADS)
    dtb_row = _pad_lanes(dn_dt_bias[0].reshape(1, -1).astype(F32), DN_HEADS)
    o_dn = _gdn(dn, ba, mdn, mba, dn_conv_w[0].astype(F32), alog_row, dtb_row, row1(dn_norm_g[0]),
                batch=batch, seq=seq)
    o_sb = _sb_attn(sb, msb, batch=batch, seq=seq)

    w_r = _pad_lanes(jnp.concatenate(
        [router_group_w[0], router_expert_w[0].transpose(1, 0, 2).reshape(d, N_EXPERTS)], axis=1)).astype(BF16)
    b_r = _pad_lanes(jnp.concatenate(
        [router_group_b[0].reshape(1, -1), router_expert_b[0].reshape(1, -1)], axis=1).astype(F32))
    h1, comb, cnt = _mix_out(x2, o_dn, o_sb, g0, b0, w_zg, b_gate[0].astype(F32), w_branch_dn[0].astype(BF16),
                             w_branch_sb[0].astype(BF16), w_out[0].astype(BF16), row1(ln1_g[0]),
                             row1(ln1_b[0]), w_r, b_r, tm=MOE_TM)

    wg = expert_w_gate[0].astype(BF16)
    wu = expert_w_up[0].astype(BF16)
    wd = expert_w_down[0].reshape(N_GROUPS, EXPERTS_PER_GROUP * EXPERT_FF, d).astype(BF16)
    tabs, n_ffn_tiles = _moe_tables(cnt[:, 0, :N_GROUPS].astype(jnp.int32), rows)
    seg_tabs = (tabs["seg"], tabs["soff"], tabs["doff"])
    xg = _dispatch(h1, comb, seg_tabs + (tabs["tlen"], tabs["toff"], tabs["n_valid"]), n_ffn_tiles=n_ffn_tiles)
    ys = _group_ffn(xg, tabs["tile_group"], tabs["n_valid"], wg, wu, wd)
    out = _combine(h1, comb, ys, row1(ln2_g[0]), row1(ln2_b[0]), seg_tabs)
    return out.reshape(batch, seq, d)
```

```python
import functools

import jax
import jax.numpy as jnp
import numpy as np
from jax import lax
from jax.experimental import pallas as pl
from jax.experimental.pallas import tpu as pltpu

F32 = jnp.float32
BF16 = jnp.bfloat16

D_MODEL = 1024
N_META = 16
DN_HEADS = 4
DN_HEAD_DIM = 128
DN_WIDTH = DN_HEADS * DN_HEAD_DIM
DN_CONV = 4
DN_CHUNK = 64
SB_HEADS = 8
SB_HEAD_DIM = 64
SB_WIDTH = SB_HEADS * SB_HEAD_DIM
N_GROUPS = 4
EXPERTS_PER_GROUP = 8
N_EXPERTS = N_GROUPS * EXPERTS_PER_GROUP
EXPERT_FF = 256
DEEPNORM_ALPHA = 2.0 ** 0.25
LN_EPS = 1e-5
RMS_EPS = 1e-6

LANES = 128
SUBLANES = 8
META_ROWS = DN_CHUNK
META_PAD = META_ROWS - N_META
ROUTER_COL0 = N_GROUPS
VMEM_LIMIT = 56 * 1024 * 1024


def _layer_norm(x, g, b):
    mu = jnp.mean(x, -1, keepdims=True)
    xc = x - mu
    var = jnp.mean(xc * xc, -1, keepdims=True)
    return xc * lax.rsqrt(var + LN_EPS) * g + b


def _sigmoid(x):
    return 1.0 / (1.0 + jnp.exp(-x))


def _softplus(x):
    return jnp.maximum(x, 0.0) + jnp.log(1.0 + jnp.exp(-jnp.abs(x)))


def _silu(x):
    return x * _sigmoid(x)


def _dot(a, b):
    return jnp.dot(a, b, preferred_element_type=F32)


def _dot_nt(a, b):
    return lax.dot_general(a, b, (((1,), (1,)), ((), ())), preferred_element_type=F32)


def _dot_tn(a, b):
    return lax.dot_general(a, b, (((0,), (0,)), ((), ())), preferred_element_type=F32)


def _ln_proj_kernel(x_ref, g_ref, b_ref, wdn_ref, wsb_ref, wba_ref, dn_ref, sb_ref, ba_ref, *, n_zero):
    h = _layer_norm(x_ref[...], g_ref[...], b_ref[...])
    if n_zero:
        rows = lax.broadcasted_iota(jnp.int32, h.shape, 0)
        h = jnp.where(rows >= n_zero, h, 0.0)
    hb = h.astype(BF16)
    dn_ref[...] = _dot(hb, wdn_ref[...])
    sb_ref[...] = _dot(hb, wsb_ref[...]).astype(BF16)
    ba_ref[...] = _dot(hb, wba_ref[...])


def _ln_proj(x2, g, b, wdn, wsb, wba, *, tm, n_zero=0):
    rows = x2.shape[0]
    const = lambda i: (0, 0)
    row = lambda i: (i, 0)
    return pl.pallas_call(
        functools.partial(_ln_proj_kernel, n_zero=n_zero),
        grid=(rows // tm,),
        in_specs=[
            pl.BlockSpec((tm, D_MODEL), row),
            pl.BlockSpec((1, D_MODEL), const),
            pl.BlockSpec((1, D_MODEL), const),
            pl.BlockSpec(wdn.shape, const),
            pl.BlockSpec(wsb.shape, const),
            pl.BlockSpec(wba.shape, const),
        ],
        out_specs=[
            pl.BlockSpec((tm, 3 * DN_WIDTH), row),
            pl.BlockSpec((tm, 3 * SB_WIDTH), row),
            pl.BlockSpec((tm, LANES), row),
        ],
        out_shape=[
            jax.ShapeDtypeStruct((rows, 3 * DN_WIDTH), F32),
            jax.ShapeDtypeStruct((rows, 3 * SB_WIDTH), BF16),
            jax.ShapeDtypeStruct((rows, LANES), F32),
        ],
        compiler_params=pltpu.CompilerParams(dimension_semantics=("arbitrary",), vmem_limit_bytes=VMEM_LIMIT),
        name="ln_proj",
    )(x2, g, b, wdn, wsb, wba)


GDN_BLOCK = 256
GDN_STREAMS = 1
GDN_CHUNK = DN_CHUNK
CONV_HIST = 8


def _gdn_rows(streams, n, n_zero, cw_ref, alog_ref, dtb_ref, ng_ref):
    c = min(n, GDN_CHUNK)
    ri = lax.broadcasted_iota(jnp.int32, (n, n), 0)
    ci = lax.broadcasted_iota(jnp.int32, (n, n), 1)
    same = (ri // c) == (ci // c)
    causal = same & (ri >= ci)
    strict = same & (ri > ci)
    tril = causal.astype(BF16)

    pre = []
    for src_ref, ba_ref, _, xbuf, _, vn_ref in streams:
        xbuf[CONV_HIST:CONV_HIST + n, :] = src_ref[...]
        acc = xbuf[CONV_HIST:CONV_HIST + n, :] * cw_ref[DN_CONV - 1:DN_CONV, :]
        for i in range(DN_CONV - 1):
            s = DN_CONV - 1 - i
            acc = acc + xbuf[CONV_HIST - s:CONV_HIST - s + n, :] * cw_ref[i:i + 1, :]
        hist = xbuf[n:n + CONV_HIST, :]
        xbuf[0:CONV_HIST, :] = hist
        qkv = _silu(acc)

        ba = ba_ref[...]
        beta_all = _sigmoid(ba)
        g_all = -jnp.exp(alog_ref[...]) * _softplus(ba + dtb_ref[...])
        if n_zero:
            rows = lax.broadcasted_iota(jnp.int32, g_all.shape, 0)
            g_all = jnp.where(rows >= n_zero, g_all, 0.0)
        g_hi = g_all.astype(BF16)
        g_lo = (g_all - g_hi.astype(F32)).astype(BF16)
        dec = _dot(tril, g_hi) + _dot(tril, g_lo)
        vn_ref[...] = jnp.zeros_like(vn_ref)
        pre.append((qkv, beta_all, dec, dec.T))

    units = [(si, h) for si in range(len(streams)) for h in range(DN_HEADS)]
    qe, ks, d_cols, qks, rhss, ps = [], [], [], [], [], []
    for si, h in units:
        qkv, beta_all, dec, dec_t = pre[si]
        q = qkv[:, h * DN_HEAD_DIM:(h + 1) * DN_HEAD_DIM]
        k = qkv[:, DN_WIDTH + h * DN_HEAD_DIM:DN_WIDTH + (h + 1) * DN_HEAD_DIM]
        v = qkv[:, 2 * DN_WIDTH + h * DN_HEAD_DIM:2 * DN_WIDTH + (h + 1) * DN_HEAD_DIM]
        q = q * lax.rsqrt(jnp.sum(q * q, -1, keepdims=True) + RMS_EPS) * (DN_HEAD_DIM ** -0.5)
        k = k * lax.rsqrt(jnp.sum(k * k, -1, keepdims=True) + RMS_EPS)
        beta = beta_all[:, h:h + 1]
        d_col = dec[:, DN_HEADS + h:DN_HEADS + h + 1]
        d_row = dec_t[DN_HEADS + h:DN_HEADS + h + 1, :]
        lmask = jnp.where(causal, jnp.exp(jnp.where(causal, d_col - d_row, 0.0)), 0.0)
        kb = k * beta
        k16 = k.astype(BF16)
        a = jnp.where(strict, _dot_nt(kb.astype(BF16), k16) * lmask, 0.0)
        qks.append((_dot_nt(q.astype(BF16), k16) * lmask).astype(BF16))
        e_col = jnp.exp(d_col)
        rhss.append(jnp.concatenate([v * beta, kb * e_col], axis=1))
        qe.append(q * e_col)
        ks.append(k)
        d_cols.append(d_col)
        ps.append(-a)
    ts = list(ps)
    for _ in range(c.bit_length() - 2):
        p16s = [p.astype(BF16) for p in ps]
        ps = [_dot(p16, p16) for p16 in p16s]
        ts = [t + p + _dot(t.astype(BF16), p.astype(BF16)) for t, p in zip(ts, ps)]
    uws = [rhs + _dot(t.astype(BF16), rhs.astype(BF16)) for t, rhs in zip(ts, rhss)]

    for ch in range(n // c):
        r0 = ch * c
        for u, (si, h) in enumerate(units):
            _, _, o_ref, _, s_ref, vn_ref = streams[si]
            k, d_col, qk, uw = ks[u], d_cols[u], qks[u], uws[u]
            d_c = d_col[r0:r0 + c, :]
            d_last = d_col[r0 + c - 1:r0 + c, :]
            s = s_ref[h]
            wq = jnp.concatenate([uw[r0:r0 + c, DN_HEAD_DIM:], qe[u][r0:r0 + c, :]], axis=0)
            r = _dot(wq.astype(BF16), s.astype(BF16))
            v_new = uw[r0:r0 + c, :DN_HEAD_DIM] - r[:c, :]
            vn16 = v_new.astype(BF16)
            vn_ref[h, r0:r0 + c, :] = vn16
            o = r[c:, :] + _dot(qk[r0:r0 + c, :], vn_ref[h])
            k_dec = k[r0:r0 + c, :] * jnp.exp(d_last - d_c)
            s_ref[h] = s * jnp.exp(d_last) + _dot_tn(k_dec.astype(BF16), vn16)
            if o_ref is not None:
                o_n = o * lax.rsqrt(jnp.mean(o * o, -1, keepdims=True) + RMS_EPS) * ng_ref[...]
                o_ref[r0:r0 + c, h * DN_HEAD_DIM:(h + 1) * DN_HEAD_DIM] = o_n


def _gdn_kernel(dn_ref, ba_ref, mdn_ref, mba_ref, cw_ref, alog_ref, dtb_ref, ng_ref, o_ref, xbuf, s_ref, vn_ref,
                vnm_ref):
    @pl.when(pl.program_id(1) == 0)
    def _():
        s_ref[0] = jnp.zeros(s_ref.shape[1:], F32)
        xbuf[0, 0:CONV_HIST, :] = jnp.zeros((CONV_HIST, xbuf.shape[2]), F32)
        _gdn_rows([(mdn_ref, mba_ref, None, xbuf.at[0], s_ref.at[0], vnm_ref)], META_ROWS, META_PAD,
                  cw_ref, alog_ref, dtb_ref, ng_ref)
        for si in range(1, GDN_STREAMS):
            s_ref[si] = s_ref[0]
            xbuf[si, 0:CONV_HIST, :] = xbuf[0, 0:CONV_HIST, :]

    _gdn_rows([(dn_ref.at[si], ba_ref.at[si], o_ref.at[si], xbuf.at[si], s_ref.at[si], vn_ref.at[si])
               for si in range(GDN_STREAMS)], GDN_BLOCK, 0, cw_ref, alog_ref, dtb_ref, ng_ref)


def _gdn(dn, ba, mdn, mba, conv_w, alog_row, dtb_row, norm_g, *, batch, seq):
    nb = seq // GDN_BLOCK
    const = lambda b, j: (0, 0)
    row = lambda b, j: (b, j, 0)
    dn = dn.reshape(batch, seq, dn.shape[-1])
    ba = ba.reshape(batch, seq, ba.shape[-1])
    out = pl.pallas_call(
        _gdn_kernel,
        grid=(batch // GDN_STREAMS, nb),
        in_specs=[
            pl.BlockSpec((GDN_STREAMS, GDN_BLOCK, 3 * DN_WIDTH), row),
            pl.BlockSpec((GDN_STREAMS, GDN_BLOCK, LANES), row),
            pl.BlockSpec((META_ROWS, 3 * DN_WIDTH), const),
            pl.BlockSpec((META_ROWS, LANES), const),
            pl.BlockSpec((DN_CONV, 3 * DN_WIDTH), const),
            pl.BlockSpec((1, LANES), const),
            pl.BlockSpec((1, LANES), const),
            pl.BlockSpec((1, DN_HEAD_DIM), const),
        ],
        out_specs=pl.BlockSpec((GDN_STREAMS, GDN_BLOCK, DN_WIDTH), row),
        out_shape=jax.ShapeDtypeStruct((batch, seq, DN_WIDTH), F32),
        scratch_shapes=[
            pltpu.VMEM((GDN_STREAMS, GDN_BLOCK + CONV_HIST, 3 * DN_WIDTH), F32),
            pltpu.VMEM((GDN_STREAMS, DN_HEADS, DN_HEAD_DIM, DN_HEAD_DIM), F32),
            pltpu.VMEM((GDN_STREAMS, DN_HEADS, GDN_BLOCK, DN_HEAD_DIM), BF16),
            pltpu.VMEM((DN_HEADS, META_ROWS, DN_HEAD_DIM), BF16),
        ],
        compiler_params=pltpu.CompilerParams(dimension_semantics=("arbitrary", "arbitrary"),
                                             vmem_limit_bytes=VMEM_LIMIT),
        name="gdn",
    )(dn, ba, mdn, mba, conv_w, alog_row, dtb_row, norm_g)
    return out.reshape(batch * seq, DN_WIDTH)


SB_T = 2 * LANES
HEADS_PER_BLOCK = LANES // SB_HEAD_DIM
SB_SLOTS = 4
SB_UNROLL = 16
NEG_BIG = -1e30


def _sb_schedule(nq):
    qoff, koff, bsel, first = [], [], [], []
    for qi in range(nq):
        for kj in range(qi, -1, -1):
            qoff.append(qi * SB_T)
            koff.append(kj * SB_T)
            bsel.append(1 if kj == qi else 0)
            first.append(1 if kj == qi else 0)
    return [np.asarray(a, np.int32) for a in (qoff, koff, bsel, first)]


def _sb_kernel(qoff_ref, koff_ref, bsel_ref, first_ref, q_ref, k_ref, v_ref, mk_ref, mv_ref, o_ref,
               qm_s, bias_s, nu_s, z_s, sp_s, later_s, w_s, acc_s, carry_s, accq_s, carryq_s, *, n_tiles):
    t = SB_T
    ri = lax.broadcasted_iota(jnp.int32, (t, t), 0)
    ci = lax.broadcasted_iota(jnp.int32, (t, t), 1)
    bias_s[0] = jnp.zeros((t, t), F32)
    bias_s[1] = jnp.where(ci < ri, 0.0, NEG_BIG)
    acc_s[...] = jnp.zeros_like(acc_s)
    carry_s[...] = jnp.zeros_like(carry_s)
    nu_s[...] = jnp.where(ri > ci, -1.0, 0.0).astype(BF16)
    lane = lax.broadcasted_iota(jnp.int32, (t, LANES), 1)
    scale = SB_HEAD_DIM ** -0.5

    q_all = q_ref[...]
    lane_q = lax.broadcasted_iota(jnp.int32, q_all.shape, 1)
    for h in range(HEADS_PER_BLOCK):
        qm_s[h] = jnp.where((lane_q // SB_HEAD_DIM) == h, q_all, jnp.zeros_like(q_all)) * scale

    def st_logits(n, s):
        qo = pl.multiple_of(qoff_ref[n], t)
        ko = pl.multiple_of(koff_ref[n], t)
        kt = k_ref[pl.ds(ko, t), :]
        bias = bias_s[bsel_ref[n]]
        for h in range(HEADS_PER_BLOCK):
            z_s[s, h] = _dot_nt(qm_s[h, pl.ds(qo, t), :], kt) + bias

    def st_softplus(n, s):
        for h in range(HEADS_PER_BLOCK):
            z = z_s[s, h]
            sp = jnp.maximum(z, 0.0) + jnp.log(1.0 + jnp.exp(-jnp.abs(z)))
            sp_s[s, h] = sp.astype(BF16)
            z_s[s, h] = z - sp

    def st_keysum(n, s):
        for h in range(HEADS_PER_BLOCK):
            later_s[s, h] = _dot(sp_s[s, h], nu_s[...])

    def st_weights(n, s):
        keep = jnp.where(first_ref[n] == 1, 0.0, 1.0)
        qb = qoff_ref[n] // t
        for h in range(HEADS_PER_BLOCK):
            later = later_s[s, h]
            carry = carry_s[h] * keep
            logw = z_s[s, h] + later + jnp.concatenate([carry] * (t // LANES), axis=1)
            w_s[s, h] = jnp.exp(logw).astype(BF16)
            total = later[:, 0:1] - sp_s[s, h, :, 0:1].astype(F32)
            carry = carry + jnp.broadcast_to(total, (t, LANES))
            carry_s[h] = carry
            carryq_s[qb, h] = carry

    def st_values(n, s):
        ko = pl.multiple_of(koff_ref[n], t)
        vt = v_ref[pl.ds(ko, t), :]
        keep = jnp.where(first_ref[n] == 1, 0.0, 1.0)
        qb = qoff_ref[n] // t
        for h in range(HEADS_PER_BLOCK):
            acc = acc_s[h] * keep + _dot(w_s[s, h], vt)
            acc_s[h] = acc
            accq_s[qb, h] = acc

    stages = (st_logits, st_softplus, st_keysum, st_weights, st_values)
    depth = len(stages)

    def trip(it, parity):
        for d in reversed(range(depth)):
            n = it - d
            if isinstance(n, int) and not 0 <= n < n_tiles:
                continue
            stages[d](n, (parity - d) % SB_SLOTS)

    for it in range(depth - 1):
        trip(it, it % SB_SLOTS)
    first_full = depth - 1
    n_full = n_tiles - first_full
    n_loop = n_full // SB_UNROLL

    def body(u, c):
        base = first_full + u * SB_UNROLL
        for j in range(SB_UNROLL):
            trip(base + j, (first_full + j) % SB_SLOTS)
        return c

    lax.fori_loop(0, n_loop, body, 0)
    for it in range(first_full + n_loop * SB_UNROLL, n_tiles + depth - 1):
        trip(it, it % SB_SLOTS)

    m = META_ROWS
    mrow = lax.broadcasted_iota(jnp.int32, (HEADS_PER_BLOCK * m, LANES), 0)
    mlane = lax.broadcasted_iota(jnp.int32, (HEADS_PER_BLOCK * m, LANES), 1)
    own = (mrow // m) == (mlane // SB_HEAD_DIM)
    mkk = jnp.where(own, jnp.concatenate([mk_ref[...]] * HEADS_PER_BLOCK, axis=0), jnp.zeros((), BF16))
    mvv = jnp.where(own, jnp.concatenate([mv_ref[...]] * HEADS_PER_BLOCK, axis=0), jnp.zeros((), BF16))
    mr = lax.broadcasted_iota(jnp.int32, (LANES, LANES), 0)
    mc = lax.broadcasted_iota(jnp.int32, (LANES, LANES), 1)
    nu_m = jnp.where((mr > mc) & ((mr // m) == (mc // m)), -1.0, 0.0).astype(BF16)
    mbias = jnp.where((lane % m) >= META_PAD, 0.0, NEG_BIG)
    head1 = (lane // SB_HEAD_DIM) == 1
    for qb in range(q_ref.shape[0] // t):
        rows = slice(qb * t, (qb + 1) * t)
        z = _dot_nt(q_ref[rows, :] * scale, mkk) + mbias
        sp = jnp.maximum(z, 0.0) + jnp.log(1.0 + jnp.exp(-jnp.abs(z)))
        later = _dot(sp.astype(BF16), nu_m)
        carry = jnp.where(head1, carryq_s[qb, 1], carryq_s[qb, 0])
        w = jnp.exp(z - sp + later + carry)
        acc = jnp.where(head1, accq_s[qb, 1], accq_s[qb, 0]) + _dot(w.astype(BF16), mvv)
        o_ref[rows, :] = acc.astype(o_ref.dtype)


def _sb_attn(sb, msb, *, batch, seq):
    nq = seq // SB_T
    n_hb = SB_WIDTH // LANES
    sched = _sb_schedule(nq)
    n_tiles = int(sched[0].shape[0])
    grid_spec = pltpu.PrefetchScalarGridSpec(
        num_scalar_prefetch=len(sched),
        grid=(batch, n_hb),
        in_specs=[
            pl.BlockSpec((seq, LANES), lambda b, hp, *_: (b, hp)),
            pl.BlockSpec((seq, LANES), lambda b, hp, *_: (b, n_hb + hp)),
            pl.BlockSpec((seq, LANES), lambda b, hp, *_: (b, 2 * n_hb + hp)),
            pl.BlockSpec((META_ROWS, LANES), lambda b, hp, *_: (0, n_hb + hp)),
            pl.BlockSpec((META_ROWS, LANES), lambda b, hp, *_: (0, 2 * n_hb + hp)),
        ],
        out_specs=pl.BlockSpec((seq, LANES), lambda b, hp, *_: (b, hp)),
        scratch_shapes=[
            pltpu.VMEM((HEADS_PER_BLOCK, seq, LANES), BF16),
            pltpu.VMEM((2, SB_T, SB_T), F32),
            pltpu.VMEM((SB_T, SB_T), BF16),
            pltpu.VMEM((SB_SLOTS, HEADS_PER_BLOCK, SB_T, SB_T), F32),
            pltpu.VMEM((SB_SLOTS, HEADS_PER_BLOCK, SB_T, SB_T), BF16),
            pltpu.VMEM((SB_SLOTS, HEADS_PER_BLOCK, SB_T, SB_T), F32),
            pltpu.VMEM((SB_SLOTS, HEADS_PER_BLOCK, SB_T, SB_T), BF16),
            pltpu.VMEM((HEADS_PER_BLOCK, SB_T, LANES), F32),
            pltpu.VMEM((HEADS_PER_BLOCK, SB_T, LANES), F32),
            pltpu.VMEM((nq, HEADS_PER_BLOCK, SB_T, LANES), F32),
            pltpu.VMEM((nq, HEADS_PER_BLOCK, SB_T, LANES), F32),
        ],
    )
    return pl.pallas_call(
        functools.partial(_sb_kernel, n_tiles=n_tiles),
        grid_spec=grid_spec,
        out_shape=jax.ShapeDtypeStruct((batch * seq, SB_WIDTH), BF16),
        compiler_params=pltpu.CompilerParams(dimension_semantics=("arbitrary", "arbitrary"),
                                             vmem_limit_bytes=VMEM_LIMIT),
        name="sb_attn",
    )(*[jnp.asarray(a) for a in sched], sb, sb, sb, msb, msb)


MIX_CHAINS = 4


def _masked_lane_max(x, mask):
    return jnp.max(jnp.where(mask, x, -jnp.inf), -1, keepdims=True)


def _first_lane_eq(x, val, mask, lane):
    return jnp.min(jnp.where(mask & (x == val), lane, LANES), -1, keepdims=True)


def _route(logits):
    lane = lax.broadcasted_iota(jnp.int32, logits.shape, 1)
    gmask = lane < N_GROUPS
    gmax = _masked_lane_max(logits, gmask)
    g_idx = _first_lane_eq(logits, gmax, gmask, lane)
    g_prob = 1.0 / jnp.sum(jnp.where(gmask, jnp.exp(logits - gmax), 0.0), -1, keepdims=True)
    lo = ROUTER_COL0 + g_idx * EXPERTS_PER_GROUP
    emask = (lane >= lo) & (lane < lo + EXPERTS_PER_GROUP)
    t1 = _masked_lane_max(logits, emask)
    i1 = _first_lane_eq(logits, t1, emask, lane)
    emask2 = emask & (lane != i1)
    t2 = _masked_lane_max(logits, emask2)
    i2 = _first_lane_eq(logits, t2, emask2, lane)
    e = jnp.exp(t2 - t1)
    w1 = g_prob / (1.0 + e)
    w2 = g_prob * e / (1.0 + e)
    comb = jnp.where(lane == i1, w1, 0.0) + jnp.where(lane == i2, w2, 0.0)
    comb = jnp.where(lane == 0, g_idx.astype(F32), comb)
    counts = jnp.sum((lane == g_idx).astype(F32), 0, keepdims=True)
    return comb, counts


def _mix_out_kernel(x_ref, odn_ref, osb_ref, g0_ref, b0_ref, wzg_ref, bg_ref, wbdn_ref, wbsb_ref, wout_ref,
                    g1_ref, b1_ref, wr_ref, br_ref, h1_ref, comb_ref, cnt_ref):
    sub = x_ref.shape[0] // MIX_CHAINS
    rows = [slice(r0, r0 + sub) for r0 in range(0, x_ref.shape[0], sub)]
    h0s = [_layer_norm(x_ref[rs, :], g0_ref[...], b0_ref[...]) for rs in rows]
    zgs = [_dot(h0.astype(BF16), wzg_ref[...]) for h0 in h0s]
    bs = [_dot(osb_ref[rs, :], wbsb_ref[...]) for rs in rows]
    o_dns = [odn_ref[rs, :] * _silu(zg[:, :DN_WIDTH]) for rs, zg in zip(rows, zgs)]
    as_ = [_dot(o_dn.astype(BF16), wbdn_ref[...]) for o_dn in o_dns]
    merged = [_sigmoid(zg[:, DN_WIDTH:DN_WIDTH + D_MODEL] + bg_ref[0:1, :]) * a
              + _sigmoid(zg[:, DN_WIDTH + D_MODEL:] + bg_ref[1:2, :]) * b for zg, a, b in zip(zgs, as_, bs)]
    mixes = [_dot(m.astype(BF16), wout_ref[...]) for m in merged]
    h1s = [_layer_norm(DEEPNORM_ALPHA * h0 + mix, g1_ref[...], b1_ref[...]) for h0, mix in zip(h0s, mixes)]
    logits = [_dot(h1.astype(BF16), wr_ref[...]) + br_ref[...] for h1 in h1s]
    counts = jnp.zeros((1, LANES), F32)
    for rs, h1, lg in zip(rows, h1s, logits):
        h1_ref[rs, :] = h1
        comb, cnt = _route(lg)
        comb_ref[rs, :] = comb
        counts = counts + cnt
    cnt_ref[0] = jnp.broadcast_to(counts, cnt_ref.shape[1:])


def _mix_out(x2, o_dn, o_sb, g0, b0, wzg, bg, wbdn, wbsb, wout, g1, b1, wr, br, *, tm):
    rows = x2.shape[0]
    const = lambda i: (0, 0)
    row = lambda i: (i, 0)
    full = lambda a: pl.BlockSpec(a.shape, const)
    return pl.pallas_call(
        _mix_out_kernel,
        grid=(rows // tm,),
        in_specs=[
            pl.BlockSpec((tm, D_MODEL), row),
            pl.BlockSpec((tm, DN_WIDTH), row),
            pl.BlockSpec((tm, SB_WIDTH), row),
            full(g0), full(b0), full(wzg), full(bg), full(wbdn), full(wbsb), full(wout), full(g1), full(b1),
            full(wr), full(br),
        ],
        out_specs=[pl.BlockSpec((tm, D_MODEL), row), pl.BlockSpec((tm, LANES), row),
                   pl.BlockSpec((1, SUBLANES, LANES), lambda i: (i, 0, 0))],
        out_shape=[jax.ShapeDtypeStruct((rows, D_MODEL), F32), jax.ShapeDtypeStruct((rows, LANES), F32),
                   jax.ShapeDtypeStruct((rows // tm, SUBLANES, LANES), F32)],
        compiler_params=pltpu.CompilerParams(dimension_semantics=("arbitrary",), vmem_limit_bytes=VMEM_LIMIT),
        name="mix_out",
    )(x2, o_dn, o_sb, g0, b0, wzg, bg, wbdn, wbsb, wout, g1, b1, wr, br)


MOE_TM = 512
SEG_ALIGN = 16
SEG_SIZES = (512, 256, 128, 64, 32, 16)
SORT_ROWS = MOE_TM + 64
X_EXT = D_MODEL + 2 * LANES


def _sorted_one_hot(comb, soff_ref, i):
    tm = comb.shape[0]
    lane = lax.broadcasted_iota(jnp.int32, (tm, LANES), 1)
    onehot = lane == comb[:, 0:1].astype(jnp.int32)
    ri = lax.broadcasted_iota(jnp.int32, (tm, tm), 0)
    ci = lax.broadcasted_iota(jnp.int32, (tm, tm), 1)
    ranks = _dot((ri > ci).astype(BF16), onehot.astype(BF16))
    start = jnp.zeros((1, LANES), F32)
    for g in range(N_GROUPS):
        start = jnp.where(lane[0:1, :] == g, soff_ref[i * N_GROUPS + g].astype(F32), start)
    pos = jnp.sum(jnp.where(onehot, ranks + start, 0.0), -1, keepdims=True)
    slot = lax.broadcasted_iota(jnp.int32, (tm, SORT_ROWS), 1)
    return (slot == pos.astype(jnp.int32)).astype(BF16)


def _for_each_piece(length, fn):
    for size in SEG_SIZES:
        @pl.when((length & size) != 0)
        def _(size=size):
            fn(pl.multiple_of(length & (-2 * size), SEG_ALIGN), size)


def _segment_copies(i, len_ref, src_ref, soff_ref, dst_ref, doff_ref, sem, act):
    for g in range(N_GROUPS):
        n = i * N_GROUPS + g
        so = soff_ref[n]
        do = doff_ref[n]

        def piece(off, size, so=so, do=do):
            act(pltpu.make_async_copy(src_ref.at[pl.ds(pl.multiple_of(so + off, SEG_ALIGN), size)],
                                      dst_ref.at[pl.ds(pl.multiple_of(do + off, SEG_ALIGN), size)], sem))

        _for_each_piece(len_ref[n], piece)


def _dispatch_kernel(len_ref, soff_ref, doff_ref, tlen_ref, toff_ref, nv_ref, h1_ref, comb_ref, xg_ref, xs_ref,
                     zero_ref, sem):
    i = pl.program_id(0)
    comb = comb_ref[...]
    pt = _sorted_one_hot(comb, soff_ref, i)
    lane = lax.broadcasted_iota(jnp.int32, comb.shape, 1)
    c = jnp.where(lane >= ROUTER_COL0, comb, 0.0)
    c_hi = c.astype(BF16)
    c_lo = (c - c_hi.astype(F32)).astype(BF16)
    src = jnp.concatenate([h1_ref[...].astype(BF16), c_hi, c_lo], axis=1)
    slot = i % 2
    xs_ref[slot] = _dot_tn(pt, src).astype(BF16)

    def to_groups(tile, sl, act):
        _segment_copies(tile, len_ref, xs_ref.at[sl], soff_ref, xg_ref, doff_ref, sem.at[sl], act)

    to_groups(i, slot, lambda cp: cp.start())

    @pl.when(i > 0)
    def _():
        to_groups(i - 1, 1 - slot, lambda cp: cp.wait())

    @pl.when(i == pl.num_programs(0) - 1)
    def _():
        fill_sem = sem.at[2]
        zero_ref[...] = jnp.zeros_like(zero_ref)
        for act in (lambda cp: cp.start(), lambda cp: cp.wait()):
            for g in range(N_GROUPS):
                def piece(off, size, g=g, act=act):
                    act(pltpu.make_async_copy(
                        zero_ref.at[pl.ds(off, size)],
                        xg_ref.at[pl.ds(pl.multiple_of(toff_ref[g] + off, SEG_ALIGN), size)], fill_sem))

                _for_each_piece(tlen_ref[g], piece)

        def zero_tile(k, carry):
            cp = pltpu.make_async_copy(zero_ref, xg_ref.at[pl.ds(pl.multiple_of(k * MOE_TM, MOE_TM), MOE_TM)],
                                       fill_sem)
            cp.start()
            cp.wait()
            return carry

        lax.fori_loop(nv_ref[0], xg_ref.shape[0] // MOE_TM, zero_tile, 0)
        to_groups(i, slot, lambda cp: cp.wait())


def _dispatch(h1, comb, tabs, *, n_ffn_tiles):
    rows = h1.shape[0]
    row = lambda i, *_: (i, 0)
    grid_spec = pltpu.PrefetchScalarGridSpec(
        num_scalar_prefetch=len(tabs),
        grid=(rows // MOE_TM,),
        in_specs=[pl.BlockSpec((MOE_TM, D_MODEL), row), pl.BlockSpec((MOE_TM, LANES), row)],
        out_specs=pl.BlockSpec(memory_space=pl.ANY),
        scratch_shapes=[pltpu.VMEM((2, SORT_ROWS, X_EXT), BF16), pltpu.VMEM((MOE_TM, X_EXT), BF16),
                        pltpu.SemaphoreType.DMA((3,))],
    )
    return pl.pallas_call(
        _dispatch_kernel,
        grid_spec=grid_spec,
        out_shape=jax.ShapeDtypeStruct((n_ffn_tiles * MOE_TM, X_EXT), BF16),
        compiler_params=pltpu.CompilerParams(dimension_semantics=("arbitrary",), vmem_limit_bytes=VMEM_LIMIT),
        name="moe_dispatch",
    )(*tabs, h1, comb)


def _group_ffn_kernel(tg_ref, nv_ref, x_ref, wg_ref, wu_ref, wd_ref, y_ref):
    k = pl.program_id(0)

    @pl.when(k < nv_ref[0])
    def _():
        g = tg_ref[k]
        xe = x_ref[...]
        hb = xe[:, :D_MODEL]
        comb = xe[:, D_MODEL:D_MODEL + LANES].astype(F32) + xe[:, D_MODEL + LANES:].astype(F32)
        lane = lax.broadcasted_iota(jnp.int32, comb.shape, 1)
        parts = []
        for e in range(EXPERTS_PER_GROUP):
            col = ROUTER_COL0 + g * EXPERTS_PER_GROUP + e
            c = jnp.sum(jnp.where(lane == col, comb, 0.0), -1, keepdims=True)
            parts.append(_silu(_dot(hb, wg_ref[0, e])) * _dot(hb, wu_ref[0, e]) * c)
        y_ref[...] = _dot(jnp.concatenate(parts, axis=-1).astype(BF16), wd_ref[0])

    @pl.when(k >= nv_ref[0])
    def _():
        y_ref[...] = jnp.zeros_like(y_ref)


def _group_ffn(xg, tile_group, n_valid, wg, wu, wd):
    n_tiles = xg.shape[0] // MOE_TM
    row = lambda k, tg, nv: (jnp.minimum(k, nv[0] - 1), 0)
    out_row = lambda k, tg, nv: (k, 0)
    grp = lambda k, tg, nv: (tg[k], 0, 0)
    grid_spec = pltpu.PrefetchScalarGridSpec(
        num_scalar_prefetch=2,
        grid=(n_tiles,),
        in_specs=[
            pl.BlockSpec((MOE_TM, X_EXT), row),
            pl.BlockSpec((1,) + wg.shape[1:], lambda k, tg, nv: (tg[k], 0, 0, 0)),
            pl.BlockSpec((1,) + wu.shape[1:], lambda k, tg, nv: (tg[k], 0, 0, 0)),
            pl.BlockSpec((1,) + wd.shape[1:], grp),
        ],
        out_specs=pl.BlockSpec((MOE_TM, D_MODEL), out_row),
    )
    return pl.pallas_call(
        _group_ffn_kernel,
        grid_spec=grid_spec,
        out_shape=jax.ShapeDtypeStruct((n_tiles * MOE_TM, D_MODEL), F32),
        compiler_params=pltpu.CompilerParams(dimension_semantics=("arbitrary",), vmem_limit_bytes=VMEM_LIMIT),
        name="moe_ffn",
    )(tile_group, n_valid, xg, wg, wu, wd)


def _combine_kernel(len_ref, soff_ref, doff_ref, h1_ref, comb_ref, g2_ref, b2_ref, ys_ref, o_ref, ysb_ref, sem):
    i = pl.program_id(0)
    n = pl.num_programs(0)

    def fetch(tile, slot, act):
        _segment_copies(tile, len_ref, ys_ref, doff_ref, ysb_ref.at[slot], soff_ref, sem.at[slot], act)

    def start_fetch(tile, slot):
        ysb_ref[slot] = jnp.zeros(ysb_ref.shape[1:], F32)
        fetch(tile, slot, lambda cp: cp.start())

    @pl.when(i == 0)
    def _():
        start_fetch(0, 0)

    for slot in range(2):
        @pl.when((i + 1 < n) & ((i + 1) % 2 == slot))
        def _(slot=slot):
            start_fetch(i + 1, slot)

    pt = _sorted_one_hot(comb_ref[...], soff_ref, i)
    for slot in range(2):
        @pl.when(i % 2 == slot)
        def _(slot=slot):
            fetch(i, slot, lambda cp: cp.wait())

    ys = ysb_ref[i % 2]
    hi = ys.astype(BF16)
    lo = (ys - hi.astype(F32)).astype(BF16)
    ffn = _dot(pt, hi) + _dot(pt, lo)
    o_ref[...] = _layer_norm(DEEPNORM_ALPHA * h1_ref[...] + ffn, g2_ref[...], b2_ref[...])


def _combine(h1, comb, ys, g2, b2, tabs):
    rows = h1.shape[0]
    row = lambda i, *_: (i, 0)
    const = lambda i, *_: (0, 0)
    grid_spec = pltpu.PrefetchScalarGridSpec(
        num_scalar_prefetch=len(tabs),
        grid=(rows // MOE_TM,),
        in_specs=[pl.BlockSpec((MOE_TM, D_MODEL), row), pl.BlockSpec((MOE_TM, LANES), row),
                  pl.BlockSpec((1, D_MODEL), const), pl.BlockSpec((1, D_MODEL), const),
                  pl.BlockSpec(memory_space=pl.ANY)],
        out_specs=pl.BlockSpec((MOE_TM, D_MODEL), row),
        scratch_shapes=[pltpu.VMEM((2, SORT_ROWS, D_MODEL), F32), pltpu.SemaphoreType.DMA((2,))],
    )
    return pl.pallas_call(
        _combine_kernel,
        grid_spec=grid_spec,
        out_shape=jax.ShapeDtypeStruct((rows, D_MODEL), F32),
        compiler_params=pltpu.CompilerParams(dimension_semantics=("arbitrary",), vmem_limit_bytes=VMEM_LIMIT),
        name="moe_combine",
    )(*tabs, h1, comb, g2, b2, ys)


def _moe_tables(cnt, rows):
    n_tiles = cnt.shape[0]
    up = lambda a, m: (a + m - 1) // m * m
    seg = up(cnt, SEG_ALIGN)
    soff = jnp.cumsum(seg, axis=1) - seg
    gtot = jnp.sum(seg, axis=0)
    gpad = up(gtot, MOE_TM)
    gbase = jnp.cumsum(gpad) - gpad
    doff = gbase[None, :] + jnp.cumsum(seg, axis=0) - seg
    n_ffn_tiles = -(-(rows + n_tiles * N_GROUPS * (SEG_ALIGN - 1)) // MOE_TM) + N_GROUPS
    starts = jnp.arange(n_ffn_tiles, dtype=jnp.int32) * MOE_TM
    tile_group = jnp.minimum(jnp.sum(starts[:, None] >= (gbase + gpad)[None, :], axis=1), N_GROUPS - 1)
    n_valid = (jnp.sum(gpad) // MOE_TM).reshape(1)
    i32 = lambda a: a.reshape(-1).astype(jnp.int32)
    return dict(seg=i32(seg), soff=i32(soff), doff=i32(doff), tlen=i32(gpad - gtot), toff=i32(gbase + gtot),
                tile_group=i32(tile_group), n_valid=i32(n_valid)), n_ffn_tiles


def _pad_lanes(a, col0=0):
    return jnp.pad(a, ((0, 0), (col0, LANES - col0 - a.shape[1])))


def kernel(x, meta_tokens, ln_emb_g, ln_emb_b, w_in, b_gate, dn_conv_w, dn_a_log, dn_dt_bias, dn_norm_g,
           w_branch_dn, w_branch_sb, w_out, ln1_g, ln1_b, router_group_w, router_group_b, router_expert_w,
           router_expert_b, expert_w_gate, expert_w_up, expert_w_down, ln2_g, ln2_b):
    batch, seq, d = x.shape
    assert d == D_MODEL and w_in.shape[0] == 1 and seq % max(GDN_BLOCK, SB_T) == 0 and batch % GDN_STREAMS == 0
    rows = batch * seq
    tm = 512
    assert rows % tm == 0
    row1 = lambda a: a.reshape(1, -1).astype(F32)

    w = w_in[0]
    c0 = 3 * DN_WIDTH
    c1 = c0 + DN_WIDTH
    c2 = c1 + 2 * DN_HEADS
    c3 = c2 + 3 * SB_WIDTH
    w_dn = w[:, :c0].astype(BF16)
    w_ba = _pad_lanes(w[:, c1:c2]).astype(BF16)
    w_sb = w[:, c2:c3].astype(BF16)
    w_zg = jnp.concatenate([w[:, c0:c1], w[:, c3:]], axis=1).astype(BF16)

    x2 = x.reshape(rows, d)
    g0, b0 = row1(ln_emb_g), row1(ln_emb_b)
    dn, sb, ba = _ln_proj(x2, g0, b0, w_dn, w_sb, w_ba, tm=2 * tm)
    xm = jnp.concatenate([jnp.zeros((META_PAD, d), x.dtype), meta_tokens.astype(x.dtype)], axis=0)
    mdn, msb, mba = _ln_proj(xm, g0, b0, w_dn, w_sb, w_ba, tm=META_ROWS, n_zero=META_PAD)

    alog_row = _pad_lanes(dn_a_log[0].reshape(1, -1).astype(F32), DN_HEADS)
    dtb_row = _pad_lanes(dn_dt_bias[0].reshape(1, -1).astype(F32), DN_HEADS)
    o_dn = _gdn(dn, ba, mdn, mba, dn_conv_w[0].astype(F32), alog_row, dtb_row, row1(dn_norm_g[0]),
                batch=batch, seq=seq)
    o_sb = _sb_attn(sb, msb, batch=batch, seq=seq)

    w_r = _pad_lanes(jnp.concatenate(
        [router_group_w[0], router_expert_w[0].transpose(1, 0, 2).reshape(d, N_EXPERTS)], axis=1)).astype(BF16)
    b_r = _pad_lanes(jnp.concatenate(
        [router_group_b[0].reshape(1, -1), router_expert_b[0].reshape(1, -1)], axis=1).astype(F32))
    h1, comb, cnt = _mix_out(x2, o_dn, o_sb, g0, b0, w_zg, b_gate[0].astype(F32), w_branch_dn[0].astype(BF16),
                             w_branch_sb[0].astype(BF16), w_out[0].astype(BF16), row1(ln1_g[0]),
                             row1(ln1_b[0]), w_r, b_r, tm=MOE_TM)

    wg = expert_w_gate[0].astype(BF16)
    wu = expert_w_up[0].astype(BF16)
    wd = expert_w_down[0].reshape(N_GROUPS, EXPERTS_PER_GROUP * EXPERT_FF, d).astype(BF16)
    tabs, n_ffn_tiles = _moe_tables(cnt[:, 0, :N_GROUPS].astype(jnp.int32), rows)
    seg_tabs = (tabs["seg"], tabs["soff"], tabs["doff"])
    xg = _dispatch(h1, comb, seg_tabs + (tabs["tlen"], tabs["toff"], tabs["n_valid"]), n_ffn_tiles=n_ffn_tiles)
    ys = _group_ffn(xg, tabs["tile_group"], tabs["n_valid"], wg, wu, wd)
    out = _combine(h1, comb, ys, row1(ln2_g[0]), row1(ln2_b[0]), seg_tabs)
    return out.reshape(batch, seq, d)
```
